```python
import math
import jax, jax.numpy as jnp
from jax import lax
import numpy as np

D_MODEL = 1024
BATCH = 2
SEQ = 16384
DEPTH = 1
DEC_BATCH = 16
DEC_SEQ = 64
PAST_LEN = 4096

CHUNK = 64
D_A = D_MODEL
D_B = D_MODEL
D_MIX = D_A + D_B
CONV_A_W = 3
CONV_B_W = 4
SSM_HEADDIM = 64
SSM_HEADS = D_B // SSM_HEADDIM
SSM_GROUPS = 4
D_STATE = 128
D_XBC = D_B + 2 * SSM_GROUPS * D_STATE
SPLITS = [D_A, D_A, D_A, D_A, D_B, D_XBC, SSM_HEADS]
D_IN_PROJ = sum(SPLITS)
EPS = 1e-5

kernel_name = "hybrid_shortconv_ssd_stream_step"


def rmsnorm(x, w):
    xf = x.astype(jnp.float32)
    y = xf * lax.rsqrt(jnp.mean(xf * xf, axis=-1, keepdims=True) + EPS)
    return (y * w.astype(jnp.float32)).astype(x.dtype)


def gated_rmsnorm(y, z, w):
    return rmsnorm(y * jax.nn.silu(z), w)


def causal_dwconv(u, buf, w):
    K = w.shape[0]
    L = u.shape[1]
    up = jnp.concatenate([buf.astype(u.dtype), u], axis=1)
    y = up[:, 0:L] * w[0]
    for k in range(1, K):
        y = y + up[:, k:k + L] * w[k]
    return y, up[:, L:]


def ssd_chunked(x, dt, a, bmat, cmat, s0):
    bsz, L, H, P = x.shape
    G, N = bmat.shape[2], bmat.shape[3]
    Hg = H // G
    pad = (-L) % CHUNK
    f32 = jnp.float32
    x, dt, bmat, cmat = (t.astype(f32) for t in (x, dt, bmat, cmat))
    if pad:
        pw = lambda t: jnp.pad(t, [(0, 0), (0, pad)] + [(0, 0)] * (t.ndim - 2))
        x, dt, bmat, cmat = pw(x), pw(dt), pw(bmat), pw(cmat)
    nc = (L + pad) // CHUNK
    xdt = (x * dt[..., None]).reshape(bsz, nc, CHUNK, G, Hg, P)
    da = (dt * a.astype(f32)).reshape(bsz, nc, CHUNK, G, Hg)
    bm = bmat.reshape(bsz, nc, CHUNK, G, N)
    cm = cmat.reshape(bsz, nc, CHUNK, G, N)
    acum = jnp.cumsum(da, axis=2)
    seg = acum[:, :, :, None] - acum[:, :, None, :]
    causal = jnp.tril(jnp.ones((CHUNK, CHUNK), bool))[:, :, None, None]
    decay = jnp.exp(jnp.where(causal, seg, -jnp.inf))
    cb = jnp.einsum('bcign,bcjgn->bcijg', cm, bm)
    y_diag = jnp.einsum('bcijg,bcijgh,bcjghp->bcighp', cb, decay, xdt)
    decay_end = jnp.exp(acum[:, :, -1:] - acum)
    states = jnp.einsum('bcjgn,bcjgh,bcjghp->bcghpn', bm, decay_end, xdt)
    chunk_decay = jnp.exp(acum[:, :, -1])

    def step(s, inp):
        st, dec = inp
        return s * dec[..., None, None] + st, s

    s_final, s_in = lax.scan(step, s0.astype(f32).reshape(bsz, G, Hg, P, N),
                             (jnp.moveaxis(states, 1, 0), jnp.moveaxis(chunk_decay, 1, 0)))
    s_in = jnp.moveaxis(s_in, 0, 1)
    y_off = jnp.einsum('bcign,bcghpn,bcigh->bcighp', cm, s_in, jnp.exp(acum))
    y = (y_diag + y_off).reshape(bsz, nc * CHUNK, H, P)[:, :L]
    return y, s_final.reshape(bsz, H, P, N)


def mixer_layer(x, c, buf_a, buf_b, s0, w_mod, b_mod, norm_in_w, w_in, conv_a_w, norm_a_w,
                conv_b_w, conv_b_b, dt_bias, a_log, d_skip, norm_b_w, w_out):
    bsz, L, _ = x.shape
    mod = c @ w_mod + b_mod
    shift, scale, gate = jnp.split(mod, 3, axis=-1)
    h = rmsnorm(x, norm_in_w) * (1 + scale[:, None]) + shift[:, None]
    proj = h @ w_in
    idx = list(np.cumsum(SPLITS)[:-1])
    b_gate, c_gate, h_a, g_a, z, xbc_raw, dt_raw = jnp.split(proj, idx, axis=-1)
    u = c_gate * h_a
    conv_u, new_a = causal_dwconv(u, buf_a, conv_a_w)
    y_a = gated_rmsnorm(b_gate * conv_u, g_a, norm_a_w)
    xbc, new_b = causal_dwconv(xbc_raw, buf_b, conv_b_w)
    xbc = jax.nn.silu(xbc + conv_b_b)
    xs, bs, cs = jnp.split(xbc, [D_B, D_B + SSM_GROUPS * D_STATE], axis=-1)
    xs = xs.reshape(bsz, L, SSM_HEADS, SSM_HEADDIM)
    dt = jax.nn.softplus(dt_raw.astype(jnp.float32) + dt_bias.astype(jnp.float32))
    a = -jnp.exp(a_log.astype(jnp.float32))
    y, s_new = ssd_chunked(xs, dt, a,
                           bs.reshape(bsz, L, SSM_GROUPS, D_STATE),
                           cs.reshape(bsz, L, SSM_GROUPS, D_STATE), s0)
    y = y.astype(x.dtype) + d_skip[:, None] * xs
    y_b = gated_rmsnorm(y.reshape(bsz, L, D_B), z, norm_b_w)
    out = jnp.concatenate([y_a, y_b], axis=-1) @ w_out
    return x + gate[:, None] * out, new_a, new_b, s_new.astype(s0.dtype)


def trunk(x, c, sa, sb, ss, w_mod, b_mod, norm_in_w, w_in, conv_a_w, norm_a_w, conv_b_w,
          conv_b_b, dt_bias, a_log, d_skip, norm_b_w, w_out, norm_f_w):
    new_a, new_b, new_s = [], [], []
    for l in range(DEPTH):
        x, na, nb, ns = mixer_layer(x, c, sa[l], sb[l], ss[l], w_mod[l], b_mod[l], norm_in_w[l],
                                    w_in[l], conv_a_w[l], norm_a_w[l], conv_b_w[l], conv_b_b[l],
                                    dt_bias[l], a_log[l], d_skip[l], norm_b_w[l], w_out[l])
        new_a.append(na)
        new_b.append(nb)
        new_s.append(ns)
    return rmsnorm(x, norm_f_w), jnp.stack(new_a), jnp.stack(new_b), jnp.stack(new_s)


def setup_inputs(seed: int = 0) -> dict:
    key = jax.random.key(seed)
    ks = jax.random.split(key, 24)
    f32 = jnp.float32
    nrm = lambda k, s, sc: jax.random.normal(k, s, f32) * sc
    dt0 = jnp.exp(jax.random.uniform(ks[15], (DEPTH, SSM_HEADS), f32) * (math.log(0.1) - math.log(0.001)) + math.log(0.001))
    return {
        "x_prompt": nrm(ks[0], (BATCH, SEQ, D_MODEL), 1.0),
        "x_sample": nrm(ks[1], (DEC_BATCH, DEC_SEQ, D_MODEL), 1.0),
        "state_conv_a": nrm(ks[2], (DEPTH, DEC_BATCH, CONV_A_W - 1, D_A), 0.5),
        "state_conv_b": nrm(ks[3], (DEPTH, DEC_BATCH, CONV_B_W - 1, D_XBC), 0.5),
        "state_ssm": nrm(ks[4], (DEPTH, DEC_BATCH, SSM_HEADS, SSM_HEADDIM, D_STATE), 0.1),
        "c_prompt": nrm(ks[5], (BATCH, D_MODEL), 1.0),
        "c_sample": nrm(ks[6], (DEC_BATCH, D_MODEL), 1.0),
        "w_mod": nrm(ks[7], (DEPTH, D_MODEL, 3 * D_MODEL), 0.5 * D_MODEL ** -0.5),
        "b_mod": nrm(ks[8], (DEPTH, 3 * D_MODEL), 0.02),
        "norm_in_w": 1.0 + nrm(ks[9], (DEPTH, D_MODEL), 0.02),
        "w_in": nrm(ks[10], (DEPTH, D_MODEL, D_IN_PROJ), D_MODEL ** -0.5),
        "conv_a_w": nrm(ks[11], (DEPTH, CONV_A_W, D_A), CONV_A_W ** -0.5),
        "norm_a_w": 1.0 + nrm(ks[12], (DEPTH, D_A), 0.02),
        "conv_b_w": nrm(ks[13], (DEPTH, CONV_B_W, D_XBC), CONV_B_W ** -0.5),
        "conv_b_b": nrm(ks[14], (DEPTH, D_XBC), 0.02),
        "dt_bias": dt0 + jnp.log(-jnp.expm1(-dt0)),
        "a_log": jnp.log(jax.random.uniform(ks[16], (DEPTH, SSM_HEADS), f32, 1.0, 16.0)),
        "d_skip": 1.0 + nrm(ks[17], (DEPTH, SSM_HEADS), 0.02),
        "norm_b_w": 1.0 + nrm(ks[18], (DEPTH, D_B), 0.02),
        "w_out": nrm(ks[19], (DEPTH, D_MIX, D_MODEL), D_MIX ** -0.5),
        "norm_f_w": 1.0 + nrm(ks[20], (D_MODEL,), 0.02),
    }


def reference(x_prompt, x_sample, state_conv_a, state_conv_b, state_ssm, c_prompt, c_sample,
              w_mod, b_mod, norm_in_w, w_in, conv_a_w, norm_a_w, conv_b_w, conv_b_b,
              dt_bias, a_log, d_skip, norm_b_w, w_out, norm_f_w):
    dt_ = x_prompt.dtype
    za = jnp.zeros((DEPTH, x_prompt.shape[0], CONV_A_W - 1, D_A), dt_)
    zb = jnp.zeros((DEPTH, x_prompt.shape[0], CONV_B_W - 1, D_XBC), dt_)
    zs = jnp.zeros((DEPTH, x_prompt.shape[0], SSM_HEADS, SSM_HEADDIM, D_STATE), state_ssm.dtype)
    y_prompt, conv_a_p, conv_b_p, ssm_p = trunk(
        x_prompt, c_prompt, za, zb, zs, w_mod, b_mod, norm_in_w, w_in, conv_a_w, norm_a_w,
        conv_b_w, conv_b_b, dt_bias, a_log, d_skip, norm_b_w, w_out, norm_f_w)
    y_sample, conv_a_s, conv_b_s, ssm_s = trunk(
        x_sample, c_sample, state_conv_a, state_conv_b, state_ssm, w_mod, b_mod, norm_in_w,
        w_in, conv_a_w, norm_a_w, conv_b_w, conv_b_b, dt_bias, a_log, d_skip, norm_b_w,
        w_out, norm_f_w)
    return (y_prompt, y_sample, conv_a_p, conv_b_p, ssm_p, conv_a_s, conv_b_s, ssm_s)
```

```python
import functools

import jax
import jax.numpy as jnp
import numpy as np
from jax import lax
from jax.experimental import pallas as pl
from jax.experimental.pallas import tpu as pltpu

D_MODEL = 1024
D_A = 1024
D_B = 1024
CONV_A_W = 3
CONV_B_W = 4
HEADS = 16
HEADDIM = 64
GROUPS = 4
HEADS_PER_GROUP = HEADS // GROUPS
D_STATE = 128
D_BC = GROUPS * D_STATE
D_XBC = D_B + 2 * D_BC
GROUP_W = HEADS_PER_GROUP * HEADDIM
EPS = 1e-5

LANES = 128
HIST = 8
DT_COPIES = 3
DT_W = LANES

OFF_A = 0
OFF_Z = 4 * D_A
OFF_XBC = OFF_Z + D_B
OFF_DT = OFF_XBC + D_XBC
D_IN_PAD = OFF_DT + DT_W

VMEM_LIMIT_BYTES = 56 * 1024 * 1024

_F32 = jnp.float32
_BF16 = jnp.bfloat16


def _dot(a, b):
    return jnp.dot(a, b, preferred_element_type=_F32)


def _dot_nt(a, b):
    return lax.dot_general(a, b, (((1,), (1,)), ((), ())), preferred_element_type=_F32)


def _dot_tn(a, b):
    return lax.dot_general(a, b, (((0,), (0,)), ((), ())), preferred_element_type=_F32)


def _rms_scale(v):
    return lax.rsqrt(jnp.mean(v * v, axis=-1, keepdims=True) + EPS)


def _split3(v):
    hi = v.astype(_BF16)
    r1 = v - hi.astype(_F32)
    mid = r1.astype(_BF16)
    lo = (r1 - mid.astype(_F32)).astype(_BF16)
    lane = lax.broadcasted_iota(jnp.int32, v.shape, 1)
    return jnp.where(lane < HEADS, hi, jnp.where(lane < 2 * HEADS, mid, lo))


def _mod_kernel(c_ref, w_ref, b_ref, o_ref):
    o_ref[...] = _dot(c_ref[...], w_ref[...]) + b_ref[...]


def _layer_kernel(x_ref, mod_ref, sa_ref, sb_ref, ss_ref,
                  nin_ref, win_ref, caw_ref, naw_ref, cbw_ref, cbb_ref, dtb_ref, alog_ref,
                  dsk_ref, nbw_ref, wout_ref, nf_ref, e64_ref, eq_ref, tri_ref,
                  y_ref, oa_ref, ob_ref, os_ref,
                  ubuf, xbuf, st_ref, *, S, Lb, Q):
    t = pl.program_id(1)
    R = S * Lb

    @pl.when(t == 0)
    def _load_carried_state():
        for s in range(S):
            ubuf[s, HIST - (CONV_A_W - 1):HIST, :] = sa_ref[s]
            xbuf[s, HIST - (CONV_B_W - 1):HIST, :] = sb_ref[s]
            for g in range(GROUPS):
                st_ref[s, g] = ss_ref[s, GROUP_W * g:GROUP_W * (g + 1), :].T

    def per_seq(fn):
        parts = [fn(s, slice(s * Lb, (s + 1) * Lb)) for s in range(S)]
        return parts[0] if S == 1 else jnp.concatenate(parts, axis=0)

    x = x_ref[...].reshape(R, D_MODEL)

    xn = x * _rms_scale(x) * nin_ref[...]
    hb = per_seq(lambda s, rows: xn[rows] * (1.0 + mod_ref[s, 1:2, :]) + mod_ref[s, 0:1, :]).astype(_BF16)

    pa = _dot(hb, win_ref[:, OFF_A:OFF_A + 4 * D_A])
    b_gate = pa[:, 0:D_A]
    u = pa[:, D_A:2 * D_A] * pa[:, 2 * D_A:3 * D_A]
    g_a = pa[:, 3 * D_A:4 * D_A]

    def conv_a(s, rows):
        ubuf[s, HIST:HIST + Lb, :] = u[rows]
        acc = u[rows] * caw_ref[CONV_A_W - 1:CONV_A_W, :]
        for k in range(CONV_A_W - 1):
            off = HIST - (CONV_A_W - 1) + k
            acc = acc + ubuf[s, off:off + Lb, :] * caw_ref[k:k + 1, :]
        ubuf[s, HIST - (CONV_A_W - 1):HIST, :] = ubuf[s, HIST + Lb - (CONV_A_W - 1):HIST + Lb, :]
        return acc

    va = b_gate * per_seq(conv_a) * jax.nn.silu(g_a)
    ya = (va * _rms_scale(va) * naw_ref[...]).astype(_BF16)

    z = _dot(hb, win_ref[:, OFF_Z:OFF_Z + D_B])
    xr = _dot(hb, win_ref[:, OFF_XBC:OFF_XBC + D_XBC])
    dtr = _dot(hb, win_ref[:, OFF_DT:OFF_DT + DT_W])

    def conv_b(s, rows):
        xbuf[s, HIST:HIST + Lb, :] = xr[rows]
        acc = xr[rows] * cbw_ref[CONV_B_W - 1:CONV_B_W, :]
        for k in range(CONV_B_W - 1):
            off = HIST - (CONV_B_W - 1) + k
            acc = acc + xbuf[s, off:off + Lb, :] * cbw_ref[k:k + 1, :]
        xbuf[s, HIST - (CONV_B_W - 1):HIST, :] = xbuf[s, HIST + Lb - (CONV_B_W - 1):HIST + Lb, :]
        return acc

    xbc = jax.nn.silu(per_seq(conv_b) + cbb_ref[...])
    xs = xbc[:, 0:D_B]
    bm = xbc[:, D_B:D_B + D_BC].astype(_BF16)
    cm = xbc[:, D_B + D_BC:D_XBC].astype(_BF16)

    dt = jax.nn.softplus(dtr + dtb_ref[...])
    da = dt * (-jnp.exp(alog_ref[...]))
    tri = tri_ref[...]
    da_hi = da.astype(_BF16)
    da_r1 = da - da_hi.astype(_F32)
    da_mid = da_r1.astype(_BF16)
    da_lo = (da_r1 - da_mid.astype(_F32)).astype(_BF16)
    acum = _dot(tri, da_hi) + _dot(tri, da_mid) + _dot(tri, da_lo)
    nchunks = R // Q
    alast = jnp.broadcast_to(acum.reshape(nchunks, Q, LANES)[:, Q - 1:Q, :],
                             (nchunks, Q, LANES)).reshape(R, LANES)
    ea_x = _dot(_split3(jnp.exp(acum)), e64_ref[...])
    w_x = _dot(_split3(dt * jnp.exp(alast - acum)), e64_ref[...])
    a_x = _dot(_split3(acum), eq_ref[...])
    acum_t = acum.T
    dt_t = dt.T

    row_i = lax.broadcasted_iota(jnp.int32, (Q, HEADS_PER_GROUP * Q), 0)
    col_i = lax.broadcasted_iota(jnp.int32, (Q, HEADS_PER_GROUP * Q), 1)
    causal = row_i >= (col_i % Q)
    bd_row = lax.broadcasted_iota(jnp.int32, (HEADS_PER_GROUP * Q, GROUP_W), 0) // Q
    bd_col = lax.broadcasted_iota(jnp.int32, (HEADS_PER_GROUP * Q, GROUP_W), 1) // HEADDIM
    blockdiag = bd_row == bd_col

    y_rows = []
    for s in range(S):
        for c in range(Lb // Q):
            r0 = s * Lb + c * Q
            rows = slice(r0, r0 + Q)
            y_groups = []
            for g in range(GROUPS):
                gx = slice(GROUP_W * g, GROUP_W * (g + 1))
                gn = slice(D_STATE * g, D_STATE * (g + 1))
                heads = range(HEADS_PER_GROUP * g, HEADS_PER_GROUP * (g + 1))
                cg = cm[rows, gn]
                bg = bm[rows, gn]
                cb = _dot_nt(cg, bg)
                cb4 = jnp.concatenate([cb] * HEADS_PER_GROUP, axis=1)
                arow = jnp.concatenate([acum_t[h:h + 1, rows] for h in heads], axis=1)
                dtrow = jnp.concatenate([dt_t[h:h + 1, rows] for h in heads], axis=1)
                seg = a_x[rows, HEADS_PER_GROUP * Q * g:HEADS_PER_GROUP * Q * (g + 1)] - arow
                mp = cb4 * jnp.exp(jnp.where(causal, seg, -jnp.inf)) * dtrow
                xg = xs[rows, gx]
                xg_b = xg.astype(_BF16)
                bd = jnp.where(blockdiag, jnp.concatenate([xg_b] * HEADS_PER_GROUP, axis=0),
                               jnp.zeros((), _BF16))
                y_diag = _dot(mp.astype(_BF16), bd)
                st = st_ref[s, g]
                y_off = _dot(cg, st.astype(_BF16)) * ea_x[rows, gx]
                y_groups.append(y_diag + y_off)
                xw = (xg * w_x[rows, gx]).astype(_BF16)
                st_ref[s, g] = st * ea_x[r0 + Q - 1:r0 + Q, gx] + _dot_tn(bg, xw)
            y_rows.append(jnp.concatenate(y_groups, axis=1))
    y = y_rows[0] if len(y_rows) == 1 else jnp.concatenate(y_rows, axis=0)

    y = y + dsk_ref[...] * xs
    vb = y * jax.nn.silu(z)
    yb = (vb * _rms_scale(vb) * nbw_ref[...]).astype(_BF16)

    out = _dot(ya, wout_ref[0:D_A, :]) + _dot(yb, wout_ref[D_A:D_A + D_B, :])
    res = per_seq(lambda s, rows: x[rows] + mod_ref[s, 2:3, :] * out[rows])
    y_ref[...] = (res * _rms_scale(res) * nf_ref[...]).reshape(S, Lb, D_MODEL)

    @pl.when(t == pl.num_programs(1) - 1)
    def _store_carried_state():
        for s in range(S):
            oa_ref[s] = ubuf[s, HIST - (CONV_A_W - 1):HIST, :]
            ob_ref[s] = xbuf[s, HIST - (CONV_B_W - 1):HIST, :]
            for g in range(GROUPS):
                os_ref[s, GROUP_W * g:GROUP_W * (g + 1), :] = st_ref[s, g].T


def _expand_matrix(lanes_per_head):
    e = np.zeros((LANES, HEADS * lanes_per_head), np.float32)
    for k in range(DT_COPIES):
        for h in range(HEADS):
            e[k * HEADS + h, h * lanes_per_head:(h + 1) * lanes_per_head] = 1.0
    return jnp.asarray(e, _BF16)


def _chunk_tri(rows, q):
    i = np.arange(rows)
    return jnp.asarray((i[:, None] // q == i[None, :] // q) & (i[None, :] <= i[:, None]), _BF16)


def _const_spec(shape):
    return pl.BlockSpec(shape, lambda i, j: (0,) * len(shape), pipeline_mode=pl.Buffered(1))


def _layer_call(x, mod3, sa, sb, ss, params, *, S, Lb, Q):
    ns, L, _ = x.shape
    R = S * Lb
    consts = (_expand_matrix(HEADDIM), _expand_matrix(Q), _chunk_tri(R, Q))
    seq3 = lambda i, j: (i, 0, 0)
    in_specs = [
        pl.BlockSpec((S, Lb, D_MODEL), lambda i, j: (i, j, 0)),
        pl.BlockSpec((S, 3, D_MODEL), seq3),
        pl.BlockSpec((S, CONV_A_W - 1, D_A), seq3),
        pl.BlockSpec((S, CONV_B_W - 1, D_XBC), seq3),
        pl.BlockSpec((S, HEADS * HEADDIM, D_STATE), seq3),
    ] + [_const_spec(p.shape) for p in params + consts]
    out_shape = (
        jax.ShapeDtypeStruct((ns, L, D_MODEL), _F32),
        jax.ShapeDtypeStruct((ns, CONV_A_W - 1, D_A), _F32),
        jax.ShapeDtypeStruct((ns, CONV_B_W - 1, D_XBC), _F32),
        jax.ShapeDtypeStruct((ns, HEADS * HEADDIM, D_STATE), _F32),
    )
    out_specs = (
        pl.BlockSpec((S, Lb, D_MODEL), lambda i, j: (i, j, 0)),
        pl.BlockSpec((S, CONV_A_W - 1, D_A), seq3),
        pl.BlockSpec((S, CONV_B_W - 1, D_XBC), seq3),
        pl.BlockSpec((S, HEADS * HEADDIM, D_STATE), seq3),
    )
    scratch = [
        pltpu.VMEM((S, HIST + Lb, D_A), _F32),
        pltpu.VMEM((S, HIST + Lb, D_XBC), _F32),
        pltpu.VMEM((S, GROUPS, D_STATE, GROUP_W), _F32),
    ]
    return pl.pallas_call(
        functools.partial(_layer_kernel, S=S, Lb=Lb, Q=Q),
        grid=(ns // S, L // Lb),
        in_specs=in_specs,
        out_specs=out_specs,
        out_shape=out_shape,
        scratch_shapes=scratch,
        compiler_params=pltpu.CompilerParams(
            dimension_semantics=("arbitrary", "arbitrary"),
            vmem_limit_bytes=VMEM_LIMIT_BYTES),
        name=f"layer_s{S}_l{Lb}",
    )(x, mod3, sa, sb, ss, *params, *consts)


def _mod_call(c_all, w_mod, b_mod):
    rows = c_all.shape[0]
    n = w_mod.shape[1]
    bn = D_MODEL
    return pl.pallas_call(
        _mod_kernel,
        grid=(n // bn,),
        in_specs=[pl.BlockSpec((rows, D_MODEL), lambda j: (0, 0)),
                  pl.BlockSpec((D_MODEL, bn), lambda j: (0, j)),
                  pl.BlockSpec((1, bn), lambda j: (0, j))],
        out_specs=pl.BlockSpec((rows, bn), lambda j: (0, j)),
        out_shape=jax.ShapeDtypeStruct((rows, n), _F32),
        name="mod_proj",
    )(c_all, w_mod, b_mod)


def _pad_heads(v):
    return jnp.pad(jnp.tile(v, DT_COPIES), (0, DT_W - DT_COPIES * HEADS)).reshape(1, DT_W)


def kernel(x_prompt, x_sample, state_conv_a, state_conv_b, state_ssm, c_prompt, c_sample, w_mod, b_mod, norm_in_w, w_in, conv_a_w, norm_a_w, conv_b_w, conv_b_b, dt_bias, a_log, d_skip, norm_b_w, w_out, norm_f_w):
    depth = w_in.shape[0]
    assert depth == 1, "single-layer step"
    nb, nd = x_prompt.shape[0], x_sample.shape[0]

    n_rows = nb + nd
    rows_pad = -(-n_rows // 16) * 16
    c_all = jnp.pad(jnp.concatenate([c_prompt, c_sample], axis=0), ((0, rows_pad - n_rows), (0, 0)))
    mod = _mod_call(c_all.astype(_BF16), w_mod[0].astype(_BF16), b_mod[0].reshape(1, -1))
    mod3 = mod[:n_rows].reshape(n_rows, 3, D_MODEL)

    w_dt = w_in[0][:, OFF_DT:OFF_DT + HEADS]
    w_in_p = jnp.concatenate(
        [w_in[0][:, :OFF_DT]] + [w_dt] * DT_COPIES
        + [jnp.zeros((D_MODEL, DT_W - DT_COPIES * HEADS), w_in.dtype)], axis=1).astype(_BF16)
    params = (
        norm_in_w[0].reshape(1, D_MODEL), w_in_p, conv_a_w[0], norm_a_w[0].reshape(1, D_A),
        conv_b_w[0], conv_b_b[0].reshape(1, D_XBC), _pad_heads(dt_bias[0]), _pad_heads(a_log[0]),
        jnp.repeat(d_skip[0], HEADDIM).reshape(1, D_B), norm_b_w[0].reshape(1, D_B),
        w_out[0].astype(_BF16), norm_f_w.reshape(1, D_MODEL),
    )

    f32 = x_prompt.dtype
    za = jnp.zeros((nb, CONV_A_W - 1, D_A), f32)
    zb = jnp.zeros((nb, CONV_B_W - 1, D_XBC), f32)
    zs = jnp.zeros((nb, HEADS * HEADDIM, D_STATE), state_ssm.dtype)
    y_p, ca_p, cb_p, ss_p = _layer_call(x_prompt, mod3[:nb], za, zb, zs, params, S=1, Lb=256, Q=128)
    y_s, ca_s, cb_s, ss_s = _layer_call(
        x_sample, mod3[nb:], state_conv_a[0], state_conv_b[0],
        state_ssm[0].reshape(nd, HEADS * HEADDIM, D_STATE), params, S=4, Lb=x_sample.shape[1], Q=64)

    shp = lambda a: a.reshape(1, a.shape[0], HEADS, HEADDIM, D_STATE)
    return (y_p, y_s, ca_p[None], cb_p[None], shp(ss_p), ca_s[None], cb_s[None], shp(ss_s))
```

```python
import functools

import jax
import jax.numpy as jnp
import numpy as np
from jax import lax
from jax.experimental import pallas as pl
from jax.experimental.pallas import tpu as pltpu

D_MODEL = 1024
D_A = 1024
D_B = 1024
CONV_A_W = 3
CONV_B_W = 4
HEADS = 16
HEADDIM = 64
GROUPS = 4
HEADS_PER_GROUP = HEADS // GROUPS
D_STATE = 128
D_BC = GROUPS * D_STATE
D_XBC = D_B + 2 * D_BC
GROUP_W = HEADS_PER_GROUP * HEADDIM
EPS = 1e-5

LANES = 128
SUBLANES = 8
MXU_W = 256
ROW_BLOCK = 32
DT_COPIES = 3
DT_W = LANES

OFF_A = 0
OFF_Z = 4 * D_A
OFF_XBC = OFF_Z + D_B
D_IN_MAIN = OFF_XBC + D_XBC - MXU_W
D_IN_TAIL = MXU_W + DT_W

VMEM_LIMIT_BYTES = 58 * 1024 * 1024

_F32 = jnp.float32
_BF16 = jnp.bfloat16


def _dot(a, b):
    return jnp.dot(a, b, preferred_element_type=_F32)


def _dot_nt(a, b):
    return lax.dot_general(a, b, (((1,), (1,)), ((), ())), preferred_element_type=_F32)


def _dot_tn(a, b):
    return lax.dot_general(a, b, (((0,), (0,)), ((), ())), preferred_element_type=_F32)


def _sumsq(v):
    return jnp.sum(v * v, axis=-1, keepdims=True)


def _split3(v):
    hi = v.astype(_BF16)
    r1 = v - hi.astype(_F32)
    mid = r1.astype(_BF16)
    lo = (r1 - mid.astype(_F32)).astype(_BF16)
    lane = lax.broadcasted_iota(jnp.int32, v.shape, 1)
    return jnp.where(lane < HEADS, hi, jnp.where(lane < 2 * HEADS, mid, lo))


def _col_chunks(width):
    return [slice(c, c + MXU_W) for c in range(0, width, MXU_W)]


def _row_blocks(S, Lb):
    return [(s, rb, s * Lb + rb * ROW_BLOCK) for s in range(S) for rb in range(Lb // ROW_BLOCK)]


def _in_proj_chunks(x_rows, mod_ref, nin_ref, wmain_ref, wtail_ref, hb_ref, slot, *, S, Lb):
    pa_ref, z_ref, xr_ref, dt_ref, xc_ref = slot
    for s, _, r0 in _row_blocks(S, Lb):
        rows = slice(r0, r0 + ROW_BLOCK)
        x = x_rows(rows)
        xc_ref[rows, :] = x
        xn = x * lax.rsqrt(_sumsq(x) * (1.0 / D_MODEL) + EPS) * nin_ref[...]
        hb_ref[rows, :] = (xn * (1.0 + mod_ref[s, 1:2, :]) + mod_ref[s, 0:1, :]).astype(_BF16)

    def plain(dst_ref, dst_cols, src_off):
        def run():
            dst_ref[:, dst_cols] = _dot(hb_ref[...], wmain_ref[:, src_off + dst_cols.start:src_off + dst_cols.stop])
        return run

    def xbc_tail_and_dt():
        r = _dot(hb_ref[...], wtail_ref[...])
        xr_ref[:, D_XBC - MXU_W:D_XBC] = r[:, 0:MXU_W]
        dt_ref[...] = r[:, MXU_W:MXU_W + DT_W]

    chunks = [plain(pa_ref, c, OFF_A) for c in _col_chunks(4 * D_A)]
    chunks += [plain(xr_ref, c, OFF_XBC) for c in _col_chunks(D_XBC - MXU_W)]
    chunks += [xbc_tail_and_dt]
    chunks += [plain(z_ref, c, OFF_Z) for c in _col_chunks(D_B)]
    return chunks


def _causal_conv(val, halo, w):
    n = val.shape[0]
    width = w.shape[0]
    ext = jnp.concatenate([halo, val], axis=0)
    acc = val * w[width - 1:width]
    for k in range(width - 1):
        shifted = pltpu.roll(ext, width - 1 - k, axis=0)[SUBLANES:SUBLANES + n]
        acc = acc + shifted * w[k:k + 1]
    return acc


class _NoPacing:
    def gate(self, w):
        return w

    def done(self, value, chunks=0.0):
        pass

    def flush(self):
        pass


class _Pacing:
    LAG = 3

    def __init__(self, pending, hb_ref):
        self.pending = pending
        self.hb_ref = hb_ref
        self.zeros = []
        self.credit = 0.0

    def gate(self, w):
        if len(self.zeros) < self.LAG:
            return w
        zrow = pltpu.bitcast(self.zeros[-self.LAG], _F32)[0:1, :]
        return w + jnp.concatenate([zrow] * (w.shape[1] // LANES), axis=1)

    def done(self, value, chunks=0.0):
        rows, width = value.shape
        folded = jnp.sum(value.reshape(rows // SUBLANES, SUBLANES, width), axis=0)
        folded = sum(folded[:, c:c + LANES] for c in range(0, width, LANES))
        half = jnp.uint32(16)
        zero = lax.shift_right_logical(
            lax.shift_right_logical(pltpu.bitcast(folded, jnp.uint32), half), half)
        self.zeros.append(zero)
        self.credit += chunks
        if self.pending and self.credit >= 1.0:
            zf = pltpu.bitcast(zero, _F32)
            tile = self.hb_ref[0:2 * SUBLANES, 0:LANES].astype(_F32) + jnp.concatenate([zf, zf], axis=0)
            self.hb_ref[0:2 * SUBLANES, 0:LANES] = tile.astype(_BF16)
            while self.pending and self.credit >= 1.0:
                self.credit -= 1.0
                self.pending.pop(0)()

    def flush(self):
        while self.pending:
            self.pending.pop(0)()


def _mixers(mod_ref, slot, p, tmp, carry, pacing, emit_rows, *, S, Lb, Q):
    pa_ref, z_ref, xr_ref, dt_ref, xc_ref = slot
    (caw_ref, naw_ref, cbw_ref, cbb_ref, dtb_ref, alog_ref, dsk_ref, nbw_ref, wout_ref, nf_ref,
     e64_ref, eq_ref, tri_ref) = p
    v_ref, xs_ref, bm_ref, cm_ref, ycat_ref = tmp
    hista_ref, histb_ref, st_ref = carry
    R = S * Lb
    last_rb = Lb // ROW_BLOCK - 1
    tail = slice(ROW_BLOCK - SUBLANES, ROW_BLOCK)

    u_tail = {}
    for s, rb, r0 in _row_blocks(S, Lb):
        rows = slice(r0, r0 + ROW_BLOCK)
        ssq = jnp.zeros((ROW_BLOCK, 1), _F32)
        for cols in _col_chunks(D_A):
            sh = lambda k: slice(k * D_A + cols.start, k * D_A + cols.stop)
            u = pa_ref[rows, sh(1)] * pa_ref[rows, sh(2)]
            halo = hista_ref[s, :, cols] if rb == 0 else u_tail[cols.start]
            conv = _causal_conv(u, halo, pacing.gate(caw_ref[:, cols]))
            u_tail[cols.start] = u[tail]
            if rb == last_rb:
                hista_ref[s, :, cols] = u[tail]
            va = pa_ref[rows, sh(0)] * conv * jax.nn.silu(pa_ref[rows, sh(3)])
            v_ref[rows, cols] = va
            ssq = ssq + _sumsq(va)
            pacing.done(va, CHUNKS_PER_CONV_PIECE)
        scale = lax.rsqrt(ssq * (1.0 / D_A) + EPS)
        ycat_ref[rows, 0:D_A] = (v_ref[rows, :] * scale * naw_ref[...]).astype(_BF16)

    for s, rb, r0 in _row_blocks(S, Lb):
        rows = slice(r0, r0 + ROW_BLOCK)
        for cols in _col_chunks(D_XBC):
            xr = xr_ref[rows, cols]
            halo = histb_ref[s, :, cols] if rb == 0 else xr_ref[r0 - SUBLANES:r0, cols]
            conv = _causal_conv(xr, halo, pacing.gate(cbw_ref[:, cols]))
            if rb == last_rb:
                histb_ref[s, :, cols] = xr[tail]
            act = jax.nn.silu(conv + cbb_ref[:, cols])
            if cols.start < D_B:
                xs_ref[rows, cols] = act
            elif cols.start < D_B + D_BC:
                bm_ref[rows, cols.start - D_B:cols.stop - D_B] = act.astype(_BF16)
            else:
                cm_ref[rows, cols.start - D_B - D_BC:cols.stop - D_B - D_BC] = act.astype(_BF16)
            pacing.done(act, CHUNKS_PER_CONV_PIECE)

    dt = jax.nn.softplus(dt_ref[...] + dtb_ref[...])
    da = dt * (-jnp.exp(alog_ref[...]))
    tri = tri_ref[...]
    da_hi = da.astype(_BF16)
    da_r1 = da - da_hi.astype(_F32)
    da_mid = da_r1.astype(_BF16)
    da_lo = (da_r1 - da_mid.astype(_F32)).astype(_BF16)
    acum = _dot(tri, da_hi) + _dot(tri, da_mid) + _dot(tri, da_lo)
    nchunks = R // Q
    alast = jnp.broadcast_to(acum.reshape(nchunks, Q, LANES)[:, Q - 1:Q, :],
                             (nchunks, Q, LANES)).reshape(R, LANES)
    ea_x = _dot(_split3(jnp.exp(acum)), e64_ref[...])
    w_x = _dot(_split3(dt * jnp.exp(alast - acum)), e64_ref[...])
    a_x = _dot(_split3(acum), eq_ref[...])
    acum_t = acum.T
    dt_t = dt.T

    row_i = lax.broadcasted_iota(jnp.int32, (Q, HEADS_PER_GROUP * Q), 0)
    col_i = lax.broadcasted_iota(jnp.int32, (Q, HEADS_PER_GROUP * Q), 1)
    causal = row_i >= (col_i % Q)
    bd_row = lax.broadcasted_iota(jnp.int32, (HEADS_PER_GROUP * Q, GROUP_W), 0) // Q
    bd_col = lax.broadcasted_iota(jnp.int32, (HEADS_PER_GROUP * Q, GROUP_W), 1) // HEADDIM
    blockdiag = bd_row == bd_col

    for s in range(S):
        for c in range(Lb // Q):
            r0 = s * Lb + c * Q
            rows = slice(r0, r0 + Q)
            for g in range(GROUPS):
                gx = slice(GROUP_W * g, GROUP_W * (g + 1))
                gn = slice(D_STATE * g, D_STATE * (g + 1))
                heads = range(HEADS_PER_GROUP * g, HEADS_PER_GROUP * (g + 1))
                cg = cm_ref[rows, gn]
                bg = bm_ref[rows, gn]
                cb = _dot_nt(cg, bg)
                cb4 = jnp.concatenate([cb] * HEADS_PER_GROUP, axis=1)
                arow = jnp.concatenate([acum_t[h:h + 1, rows] for h in heads], axis=1)
                dtrow = jnp.concatenate([dt_t[h:h + 1, rows] for h in heads], axis=1)
                seg = a_x[rows, HEADS_PER_GROUP * Q * g:HEADS_PER_GROUP * Q * (g + 1)] - arow
                mp = cb4 * jnp.exp(jnp.where(causal, seg, -jnp.inf)) * dtrow
                xg = xs_ref[rows, gx]
                xg_b = xg.astype(_BF16)
                bd = jnp.where(blockdiag, jnp.concatenate([xg_b] * HEADS_PER_GROUP, axis=0),
                               jnp.zeros((), _BF16))
                y_diag = _dot(mp.astype(_BF16), bd)
                st = st_ref[s, g]
                y_off = _dot(cg, st.astype(_BF16)) * ea_x[rows, gx]
                y_g = y_diag + y_off + dsk_ref[:, gx] * xg
                v_ref[rows, gx] = y_g
                xw = (xg * w_x[rows, gx]).astype(_BF16)
                st_ref[s, g] = st * ea_x[r0 + Q - 1:r0 + Q, gx] + _dot_tn(bg, xw)
                pacing.done(y_g, CHUNKS_PER_SSD_PIECE)

    for s, rb, r0 in _row_blocks(S, Lb):
        rows = slice(r0, r0 + ROW_BLOCK)
        ssq = jnp.zeros((ROW_BLOCK, 1), _F32)
        for cols in _col_chunks(D_B):
            vb = v_ref[rows, cols] * jax.nn.silu(z_ref[rows, cols])
            v_ref[rows, cols] = vb
            ssq = ssq + _sumsq(vb)
            pacing.done(vb, CHUNKS_PER_GATE_PIECE)
        scale = lax.rsqrt(ssq * (1.0 / D_B) + EPS)
        ycat_ref[rows, D_A:D_A + D_B] = (v_ref[rows, :] * scale * nbw_ref[...]).astype(_BF16)

    out = _dot(ycat_ref[...], wout_ref[...])
    for s, rb, r0 in _row_blocks(S, Lb):
        rows = slice(r0, r0 + ROW_BLOCK)
        res = xc_ref[rows, :] + pacing.gate(mod_ref[s, 2:3, :]) * out[rows]
        y = res * lax.rsqrt(_sumsq(res) * (1.0 / D_MODEL) + EPS) * nf_ref[...]
        emit_rows(s, rb, y)
        pacing.done(y, CHUNKS_PER_OUT_PIECE)
    pacing.flush()


def _load_carried(s, src, sa_ref, sb_ref, ss_ref, carry):
    hista_ref, histb_ref, st_ref = carry
    for hist_ref, state_ref, width in ((hista_ref, sa_ref, CONV_A_W), (histb_ref, sb_ref, CONV_B_W)):
        hist_ref[s] = jnp.zeros(hist_ref.shape[1:], _F32)
        hist_ref[s, SUBLANES - (width - 1):SUBLANES, :] = state_ref[src]
    for g in range(GROUPS):
        st_ref[s, g] = ss_ref[src, GROUP_W * g:GROUP_W * (g + 1), :].T


def _store_carried(s, dst, oa_ref, ob_ref, os_ref, carry):
    hista_ref, histb_ref, st_ref = carry
    oa_ref[dst] = hista_ref[s, SUBLANES - (CONV_A_W - 1):SUBLANES, :]
    ob_ref[dst] = histb_ref[s, SUBLANES - (CONV_B_W - 1):SUBLANES, :]
    for g in range(GROUPS):
        os_ref[dst, GROUP_W * g:GROUP_W * (g + 1), :] = st_ref[s, g].T


def _mod_kernel(c_ref, w_ref, b_ref, o_ref):
    o_ref[...] = _dot(c_ref[...], w_ref[...]) + b_ref[...]


CHUNKS_PER_CONV_PIECE = 0.21
CHUNKS_PER_SSD_PIECE = 0.5
CHUNKS_PER_GATE_PIECE = 0.09
CHUNKS_PER_OUT_PIECE = 0.25

N_PARAMS = 13
N_SLOT = 5
N_TMP = 5


def _seq_kernel(x_ref, mod_ref, sa_ref, sb_ref, ss_ref, nin_ref, wmain_ref, wtail_ref, *rest, S, Lb, Q):
    p = rest[:N_PARAMS]
    y_ref, oa_ref, ob_ref, os_ref = rest[N_PARAMS:N_PARAMS + 4]
    scratch = rest[N_PARAMS + 4:]
    hb_ref, slot = scratch[0], scratch[1:1 + N_SLOT]
    tmp = scratch[1 + N_SLOT:1 + N_SLOT + N_TMP]
    carry = scratch[1 + N_SLOT + N_TMP:]
    t = pl.program_id(1)

    @pl.when(t == 0)
    def _():
        for s in range(S):
            _load_carried(s, s, sa_ref, sb_ref, ss_ref, carry)

    def x_rows(rows):
        s, r = divmod(rows.start, Lb)
        return x_ref[s, r:r + ROW_BLOCK, :]

    for run in _in_proj_chunks(x_rows, mod_ref, nin_ref, wmain_ref, wtail_ref, hb_ref, slot, S=S, Lb=Lb):
        run()
    def emit_rows(s, rb, y):
        y_ref[s, rb * ROW_BLOCK:(rb + 1) * ROW_BLOCK, :] = y

    _mixers(mod_ref, slot, p, tmp, carry, _NoPacing(), emit_rows, S=S, Lb=Lb, Q=Q)

    @pl.when(t == pl.num_programs(1) - 1)
    def _():
        for s in range(S):
            _store_carried(s, s, oa_ref, ob_ref, os_ref, carry)


def _pipe_kernel(x_ref, moda_ref, modb_ref, sa_ref, sb_ref, ss_ref, nin_ref, wmain_ref, wtail_ref, *rest,
                 Lb, Q, nt, nblocks):
    p = rest[:N_PARAMS]
    y_ref, oa_ref, ob_ref, os_ref = rest[N_PARAMS:N_PARAMS + 4]
    scratch = rest[N_PARAMS + 4:]
    hb_ref = scratch[0]
    slots = (scratch[1:1 + N_SLOT], scratch[1 + N_SLOT:1 + 2 * N_SLOT])
    tmp = scratch[1 + 2 * N_SLOT:1 + 2 * N_SLOT + N_TMP]
    carry = scratch[1 + 2 * N_SLOT + N_TMP:]
    i = pl.program_id(0)

    @pl.when(i == 0)
    def _():
        for ref in slots[1] + carry:
            ref[...] = jnp.zeros(ref.shape, ref.dtype)

    def step(wr, rd):
        pending = _in_proj_chunks(lambda rows: x_ref[0, rows, :], moda_ref, nin_ref, wmain_ref, wtail_ref,
                                  hb_ref, wr, S=1, Lb=Lb)

        def emit_rows(s, rb, y):
            y_ref[s, rb * ROW_BLOCK:(rb + 1) * ROW_BLOCK, :] = y

        _mixers(modb_ref, rd, p, tmp, carry, _Pacing(pending, hb_ref), emit_rows, S=1, Lb=Lb, Q=Q)

    @pl.when(i % 2 == 0)
    def _():
        step(slots[0], slots[1])

    @pl.when(i % 2 == 1)
    def _():
        step(slots[1], slots[0])

    @pl.when((i >= 1) & ((i - 1) % nt == nt - 1))
    def _():
        _store_carried(0, 0, oa_ref, ob_ref, os_ref, carry)

    @pl.when((i < nblocks) & (i % nt == 0))
    def _():
        _load_carried(0, 0, sa_ref, sb_ref, ss_ref, carry)


def _expand_matrix(lanes_per_head):
    e = np.zeros((LANES, HEADS * lanes_per_head), np.float32)
    for k in range(DT_COPIES):
        for h in range(HEADS):
            e[k * HEADS + h, h * lanes_per_head:(h + 1) * lanes_per_head] = 1.0
    return jnp.asarray(e, _BF16)


def _chunk_tri(rows, q):
    i = np.arange(rows)
    return jnp.asarray((i[:, None] // q == i[None, :] // q) & (i[None, :] <= i[:, None]), _BF16)


def _const_spec(shape):
    zeros = (0,) * len(shape)
    return pl.BlockSpec(shape, lambda *_: zeros, pipeline_mode=pl.Buffered(1))


def _state_shapes(ns):
    return (jax.ShapeDtypeStruct((ns, CONV_A_W - 1, D_A), _F32),
            jax.ShapeDtypeStruct((ns, CONV_B_W - 1, D_XBC), _F32),
            jax.ShapeDtypeStruct((ns, HEADS * HEADDIM, D_STATE), _F32))


def _state_specs(S, index_map):
    return [pl.BlockSpec((S, CONV_A_W - 1, D_A), index_map),
            pl.BlockSpec((S, CONV_B_W - 1, D_XBC), index_map),
            pl.BlockSpec((S, HEADS * HEADDIM, D_STATE), index_map)]


def _slot_scratch(R):
    return [pltpu.VMEM((R, 4 * D_A), _F32), pltpu.VMEM((R, D_B), _F32), pltpu.VMEM((R, D_XBC), _F32),
            pltpu.VMEM((R, DT_W), _F32), pltpu.VMEM((R, D_MODEL), _F32)]


def _tmp_scratch(R):
    return [pltpu.VMEM((R, D_A), _F32), pltpu.VMEM((R, D_B), _F32), pltpu.VMEM((R, D_BC), _BF16),
            pltpu.VMEM((R, D_BC), _BF16), pltpu.VMEM((R, D_A + D_B), _BF16)]


def _carry_scratch(S):
    return [pltpu.VMEM((S, SUBLANES, D_A), _F32),
            pltpu.VMEM((S, SUBLANES, D_XBC), _F32),
            pltpu.VMEM((S, GROUPS, D_STATE, GROUP_W), _F32)]


def _seq_call(x, mod3, sa, sb, ss, params, *, S, Lb, Q):
    ns, L, _ = x.shape
    R = S * Lb
    consts = (_expand_matrix(HEADDIM), _expand_matrix(Q), _chunk_tri(R, Q))
    seq3 = lambda i, j: (i, 0, 0)
    in_specs = ([pl.BlockSpec((S, Lb, D_MODEL), lambda i, j: (i, j, 0)),
                 pl.BlockSpec((S, 3, D_MODEL), seq3)]
                + _state_specs(S, seq3)
                + [_const_spec(a.shape) for a in params + consts])
    return pl.pallas_call(
        functools.partial(_seq_kernel, S=S, Lb=Lb, Q=Q),
        grid=(ns // S, L // Lb),
        in_specs=in_specs,
        out_specs=[pl.BlockSpec((S, Lb, D_MODEL), lambda i, j: (i, j, 0))] + _state_specs(S, seq3),
        out_shape=(jax.ShapeDtypeStruct((ns, L, D_MODEL), _F32),) + _state_shapes(ns),
        scratch_shapes=[pltpu.VMEM((R, D_MODEL), _BF16)] + _slot_scratch(R) + _tmp_scratch(R)
        + _carry_scratch(S),
        compiler_params=pltpu.CompilerParams(
            dimension_semantics=("arbitrary", "arbitrary"),
            vmem_limit_bytes=VMEM_LIMIT_BYTES),
        name="layer_seq",
    )(x, mod3, sa, sb, ss, *params, *consts)


def _pipe_call(x, mod3, sa, sb, ss, params, *, Lb, Q):
    ns, L, _ = x.shape
    nt = L // Lb
    nblocks = ns * nt
    consts = (_expand_matrix(HEADDIM), _expand_matrix(Q), _chunk_tri(Lb, Q))
    blk_a = lambda i: jnp.minimum(i, nblocks - 1)
    blk_b = lambda i: jnp.maximum(i - 1, 0)
    seq_a = lambda i: (blk_a(i) // nt, 0, 0)
    seq_b = lambda i: (blk_b(i) // nt, 0, 0)
    in_specs = ([pl.BlockSpec((1, Lb, D_MODEL), lambda i: (blk_a(i) // nt, blk_a(i) % nt, 0)),
                 pl.BlockSpec((1, 3, D_MODEL), seq_a),
                 pl.BlockSpec((1, 3, D_MODEL), seq_b)]
                + _state_specs(1, seq_a)
                + [_const_spec(a.shape) for a in params + consts])
    return pl.pallas_call(
        functools.partial(_pipe_kernel, Lb=Lb, Q=Q, nt=nt, nblocks=nblocks),
        grid=(nblocks + 1,),
        in_specs=in_specs,
        out_specs=[pl.BlockSpec((1, Lb, D_MODEL), lambda i: (blk_b(i) // nt, blk_b(i) % nt, 0))]
        + _state_specs(1, seq_b),
        out_shape=(jax.ShapeDtypeStruct((ns, L, D_MODEL), _F32),) + _state_shapes(ns),
        scratch_shapes=[pltpu.VMEM((Lb, D_MODEL), _BF16)] + _slot_scratch(Lb) + _slot_scratch(Lb)
        + _tmp_scratch(Lb) + _carry_scratch(1),
        compiler_params=pltpu.CompilerParams(
            dimension_semantics=("arbitrary",),
            vmem_limit_bytes=VMEM_LIMIT_BYTES),
        name="layer_pipe",
    )(x, mod3, mod3, sa, sb, ss, *params, *consts)


def _mod_call(c_all, w_mod, b_mod):
    rows = c_all.shape[0]
    n = w_mod.shape[1]
    bn = D_MODEL
    return pl.pallas_call(
        _mod_kernel,
        grid=(n // bn,),
        in_specs=[pl.BlockSpec((rows, D_MODEL), lambda j: (0, 0)),
                  pl.BlockSpec((D_MODEL, bn), lambda j: (0, j)),
                  pl.BlockSpec((1, bn), lambda j: (0, j))],
        out_specs=pl.BlockSpec((rows, bn), lambda j: (0, j)),
        out_shape=jax.ShapeDtypeStruct((rows, n), _F32),
        name="mod_proj",
    )(c_all, w_mod, b_mod)


def _pad_heads(v):
    return jnp.pad(jnp.tile(v, DT_COPIES), (0, DT_W - DT_COPIES * HEADS)).reshape(1, DT_W)


def kernel(x_prompt, x_sample, state_conv_a, state_conv_b, state_ssm, c_prompt, c_sample, w_mod, b_mod, norm_in_w, w_in, conv_a_w, norm_a_w, conv_b_w, conv_b_b, dt_bias, a_log, d_skip, norm_b_w, w_out, norm_f_w):
    depth = w_in.shape[0]
    assert depth == 1, "single-layer step"
    nb, nd = x_prompt.shape[0], x_sample.shape[0]

    n_rows = nb + nd
    rows_pad = -(-n_rows // 16) * 16
    c_all = jnp.pad(jnp.concatenate([c_prompt, c_sample], axis=0), ((0, rows_pad - n_rows), (0, 0)))
    mod = _mod_call(c_all.astype(_BF16), w_mod[0].astype(_BF16), b_mod[0].reshape(1, -1))
    mod3 = mod[:n_rows].reshape(n_rows, 3, D_MODEL)

    w_main = w_in[0, :, :D_IN_MAIN].astype(_BF16)
    w_dt = w_in[0, :, OFF_XBC + D_XBC:].astype(_BF16)
    w_tail = jnp.concatenate(
        [w_in[0, :, D_IN_MAIN:OFF_XBC + D_XBC].astype(_BF16)] + [w_dt] * DT_COPIES
        + [jnp.zeros((D_MODEL, DT_W - DT_COPIES * HEADS), _BF16)], axis=1)
    params = (
        norm_in_w[0].reshape(1, D_MODEL), w_main, w_tail, conv_a_w[0], norm_a_w[0].reshape(1, D_A),
        conv_b_w[0], conv_b_b[0].reshape(1, D_XBC), _pad_heads(dt_bias[0]), _pad_heads(a_log[0]),
        jnp.repeat(d_skip[0], HEADDIM).reshape(1, D_B), norm_b_w[0].reshape(1, D_B),
        w_out[0].astype(_BF16), norm_f_w.reshape(1, D_MODEL),
    )

    f32 = x_prompt.dtype
    za = jnp.zeros((nb, CONV_A_W - 1, D_A), f32)
    zb = jnp.zeros((nb, CONV_B_W - 1, D_XBC), f32)
    zs = jnp.zeros((nb, HEADS * HEADDIM, D_STATE), state_ssm.dtype)
    y_p, ca_p, cb_p, ss_p = _pipe_call(x_prompt, mod3[:nb], za, zb, zs, params, Lb=256, Q=128)
    y_s, ca_s, cb_s, ss_s = _seq_call(
        x_sample, mod3[nb:], state_conv_a[0], state_conv_b[0],
        state_ssm[0].reshape(nd, HEADS * HEADDIM, D_STATE), params, S=4, Lb=x_sample.shape[1], Q=64)

    shp = lambda a: a.reshape(1, a.shape[0], HEADS, HEADDIM, D_STATE)
    return (y_p, y_s, ca_p[None], cb_p[None], shp(ss_p), ca_s[None], cb_s[None], shp(ss_s))
```

```python
import functools

import jax
import jax.numpy as jnp
import numpy as np
from jax import lax
from jax.experimental import pallas as pl
from jax.experimental.pallas import tpu as pltpu

D_MODEL = 1024
D_A = 1024
D_B = 1024
CONV_A_W = 3
CONV_B_W = 4
HEADS = 16
HEADDIM = 64
GROUPS = 4
HEADS_PER_GROUP = HEADS // GROUPS
D_STATE = 128
D_BC = GROUPS * D_STATE
D_XBC = D_B + 2 * D_BC
D_IN_PROJ = 4 * D_A + D_B + D_XBC + HEADS
GROUP_W = HEADS_PER_GROUP * HEADDIM
EPS = 1e-5

LANES = 128
SUBLANES = 8
MXU_W = 256
ROW_BLOCK = 32
DT_COPIES = 3
DT_W = LANES

OFF_A = 0
OFF_Z = 4 * D_A
OFF_XBC = OFF_Z + D_B
OFF_DT = OFF_XBC + D_XBC

VMEM_LIMIT_BYTES = 56 * 1024 * 1024

_F32 = jnp.float32
_BF16 = jnp.bfloat16


def _dot(a, b):
    return jnp.dot(a, b, preferred_element_type=_F32)


def _dot_nt(a, b):
    return lax.dot_general(a, b, (((1,), (1,)), ((), ())), preferred_element_type=_F32)


def _dot_tn(a, b):
    return lax.dot_general(a, b, (((0,), (0,)), ((), ())), preferred_element_type=_F32)


def _sumsq(v):
    return jnp.sum(v * v, axis=-1, keepdims=True)


def _split3(v):
    hi = v.astype(_BF16)
    r1 = v - hi.astype(_F32)
    mid = r1.astype(_BF16)
    lo = (r1 - mid.astype(_F32)).astype(_BF16)
    lane = lax.broadcasted_iota(jnp.int32, v.shape, 1)
    return jnp.where(lane < HEADS, hi, jnp.where(lane < 2 * HEADS, mid, lo))


def _col_chunks(width):
    return [slice(c, c + MXU_W) for c in range(0, width, MXU_W)]


def _row_blocks(S, Lb):
    return [(s, rb, s * Lb + rb * ROW_BLOCK) for s in range(S) for rb in range(Lb // ROW_BLOCK)]


def _causal_conv(val, halo, w):
    n = val.shape[0]
    width = w.shape[0]
    ext = jnp.concatenate([halo, val], axis=0)
    acc = val * w[width - 1:width]
    for k in range(width - 1):
        shifted = pltpu.roll(ext, width - 1 - k, axis=0)[SUBLANES:SUBLANES + n]
        acc = acc + shifted * w[k:k + 1]
    return acc


class _Pacing:
    LAG = 3

    def __init__(self, chunks, hb_ref):
        self.pending = list(chunks)
        self.issued = 0
        self.hb_ref = hb_ref
        self.zeros = []
        self.credit = 0.0

    def _issue(self):
        self.pending.pop(0)()
        self.issued += 1

    def extend(self, chunks):
        self.pending += list(chunks)
        self.credit = 0.0

    def require(self, n):
        while self.issued < n:
            self._issue()

    def gate(self, w):
        if len(self.zeros) < self.LAG:
            return w
        zrow = pltpu.bitcast(self.zeros[-self.LAG], _F32)[0:1, :]
        return w + jnp.concatenate([zrow] * (w.shape[1] // LANES), axis=1)

    def done(self, value, chunks=0.0):
        rows, width = value.shape
        folded = jnp.sum(value.reshape(rows // SUBLANES, SUBLANES, width), axis=0)
        folded = sum(folded[:, c:c + LANES] for c in range(0, width, LANES))
        half = jnp.uint32(16)
        zero = lax.shift_right_logical(
            lax.shift_right_logical(pltpu.bitcast(folded, jnp.uint32), half), half)
        self.zeros.append(zero)
        self.credit += chunks
        if self.pending and self.credit >= 1.0:
            zf = pltpu.bitcast(zero, _F32)
            tile = self.hb_ref[0:2 * SUBLANES, 0:LANES].astype(_F32) + jnp.concatenate([zf, zf], axis=0)
            self.hb_ref[0:2 * SUBLANES, 0:LANES] = tile.astype(_BF16)
            while self.pending and self.credit >= 1.0:
                self.credit -= 1.0
                self._issue()


CHUNKS_PER_CONV_PIECE = 0.3
CHUNKS_PER_SSD_PIECE = 0.5
CHUNKS_AHEAD = 6


def _layer_kernel(x_ref, mod_ref, sa_ref, sb_ref, ss_ref,
                  nin_ref, win_ref, wdt_ref, caw_ref, naw_ref, cbw_ref, cbb_ref, dtb_ref, alog_ref,
                  dsk_ref, nbw_ref, wout_ref, nf_ref, e64_ref, eq_ref, tri_ref,
                  y_ref, oa_ref, ob_ref, os_ref,
                  hb_ref, pa_ref, z_ref, xr_ref, dt_ref, v_ref, xs_ref, bm_ref, cm_ref, ycat_ref,
                  hista_ref, histb_ref, st_ref, *, S, Lb, Q):
    t = pl.program_id(1)
    R = S * Lb
    blocks = _row_blocks(S, Lb)
    last_rb = Lb // ROW_BLOCK - 1
    tail = slice(ROW_BLOCK - SUBLANES, ROW_BLOCK)

    @pl.when(t == 0)
    def _load_carried_state():
        for s in range(S):
            for hist_ref, state_ref, width in ((hista_ref, sa_ref, CONV_A_W), (histb_ref, sb_ref, CONV_B_W)):
                hist_ref[s] = jnp.zeros(hist_ref.shape[1:], _F32)
                hist_ref[s, SUBLANES - (width - 1):SUBLANES, :] = state_ref[s]
            for g in range(GROUPS):
                st_ref[s, g] = ss_ref[s, GROUP_W * g:GROUP_W * (g + 1), :].T

    for s, rb, r0 in blocks:
        x = x_ref[s, rb * ROW_BLOCK:(rb + 1) * ROW_BLOCK, :]
        xn = x * lax.rsqrt(_sumsq(x) * (1.0 / D_MODEL) + EPS) * nin_ref[...]
        hb_ref[r0:r0 + ROW_BLOCK, :] = (xn * (1.0 + mod_ref[s, 1:2, :]) + mod_ref[s, 0:1, :]).astype(_BF16)

    def proj(dst_ref, dst_cols, src_off):
        def run():
            dst_ref[:, dst_cols] = _dot(hb_ref[...], win_ref[:, src_off + dst_cols.start:src_off + dst_cols.stop])
        return run

    def proj_dt():
        dt_ref[...] = _dot(hb_ref[...], wdt_ref[...])

    pa_chunks = [[proj(pa_ref, slice(k * D_A + cols.start, k * D_A + cols.stop), OFF_A) for k in range(4)]
                 for cols in _col_chunks(D_A)]
    xr_chunks = [proj(xr_ref, cols, OFF_XBC) for cols in _col_chunks(D_XBC)]
    z_chunks = [proj(z_ref, cols, OFF_Z) for cols in _col_chunks(D_B)]

    ssq = {r0: jnp.zeros((ROW_BLOCK, 1), _F32) for _, _, r0 in blocks}

    def conv_a_pieces(cols):
        sh = lambda k: slice(k * D_A + cols.start, k * D_A + cols.stop)
        u_tail = None
        for s, rb, r0 in blocks:
            rows = slice(r0, r0 + ROW_BLOCK)
            u = pa_ref[rows, sh(1)] * pa_ref[rows, sh(2)]
            halo = hista_ref[s, :, cols] if rb == 0 else u_tail
            conv = _causal_conv(u, halo, pacing.gate(caw_ref[:, cols]))
            u_tail = u[tail]
            if rb == last_rb:
                hista_ref[s, :, cols] = u_tail
            va = pa_ref[rows, sh(0)] * conv * jax.nn.silu(pa_ref[rows, sh(3)])
            v_ref[rows, cols] = va
            ssq[r0] = ssq[r0] + _sumsq(va)
            pacing.done(va, CHUNKS_PER_CONV_PIECE)

    def conv_b_pieces(cols):
        for s, rb, r0 in blocks:
            rows = slice(r0, r0 + ROW_BLOCK)
            xr = xr_ref[rows, cols]
            halo = histb_ref[s, :, cols] if rb == 0 else xr_ref[r0 - SUBLANES:r0, cols]
            conv = _causal_conv(xr, halo, pacing.gate(cbw_ref[:, cols]))
            if rb == last_rb:
                histb_ref[s, :, cols] = xr[tail]
            act = jax.nn.silu(conv + cbb_ref[:, cols])
            if cols.start < D_B:
                xs_ref[rows, cols] = act
            elif cols.start < D_B + D_BC:
                bm_ref[rows, cols.start - D_B:cols.stop - D_B] = act.astype(_BF16)
            else:
                cm_ref[rows, cols.start - D_B - D_BC:cols.stop - D_B - D_BC] = act.astype(_BF16)
            pacing.done(act, CHUNKS_PER_CONV_PIECE)

    a_cols, b_cols = _col_chunks(D_A), _col_chunks(D_XBC)
    stages = []
    for j in range(len(a_cols)):
        stages.append((pa_chunks[j], functools.partial(conv_a_pieces, a_cols[j])))
        for jb in (2 * j, 2 * j + 1):
            stages.append(([xr_chunks[jb]], functools.partial(conv_b_pieces, b_cols[jb])))
    pacing = _Pacing([c for needed, _ in stages for c in needed] + [proj_dt], hb_ref)
    pacing.require(CHUNKS_AHEAD)
    n_needed = 0
    for needed, pieces in stages:
        n_needed += len(needed)
        pacing.require(n_needed)
        pieces()
    pacing.require(n_needed + 1)
    pacing.extend(z_chunks)

    for s, rb, r0 in blocks:
        rows = slice(r0, r0 + ROW_BLOCK)
        scale = lax.rsqrt(ssq[r0] * (1.0 / D_A) + EPS)
        ycat_ref[rows, 0:D_A] = (v_ref[rows, :] * scale * naw_ref[...]).astype(_BF16)

    dt = jax.nn.softplus(dt_ref[...] + dtb_ref[...])
    da = dt * (-jnp.exp(alog_ref[...]))
    tri = tri_ref[...]
    da_hi = da.astype(_BF16)
    da_r1 = da - da_hi.astype(_F32)
    da_mid = da_r1.astype(_BF16)
    da_lo = (da_r1 - da_mid.astype(_F32)).astype(_BF16)
    acum = _dot(tri, da_hi) + _dot(tri, da_mid) + _dot(tri, da_lo)
    nchunks = R // Q
    alast = jnp.broadcast_to(acum.reshape(nchunks, Q, LANES)[:, Q - 1:Q, :],
                             (nchunks, Q, LANES)).reshape(R, LANES)
    ea_x = _dot(_split3(jnp.exp(acum)), e64_ref[...])
    w_x = _dot(_split3(dt * jnp.exp(alast - acum)), e64_ref[...])
    a_x = _dot(_split3(acum), eq_ref[...])
    acum_t = acum.T
    dt_t = dt.T

    row_i = lax.broadcasted_iota(jnp.int32, (Q, HEADS_PER_GROUP * Q), 0)
    col_i = lax.broadcasted_iota(jnp.int32, (Q, HEADS_PER_GROUP * Q), 1)
    causal = row_i >= (col_i % Q)
    bd_row = lax.broadcasted_iota(jnp.int32, (HEADS_PER_GROUP * Q, GROUP_W), 0) // Q
    bd_col = lax.broadcasted_iota(jnp.int32, (HEADS_PER_GROUP * Q, GROUP_W), 1) // HEADDIM
    blockdiag = bd_row == bd_col

    for s in range(S):
        for c in range(Lb // Q):
            r0 = s * Lb + c * Q
            rows = slice(r0, r0 + Q)
            for g in range(GROUPS):
                gx = slice(GROUP_W * g, GROUP_W * (g + 1))
                gn = slice(D_STATE * g, D_STATE * (g + 1))
                heads = range(HEADS_PER_GROUP * g, HEADS_PER_GROUP * (g + 1))
                cg = cm_ref[rows, gn]
                bg = bm_ref[rows, gn]
                cb = _dot_nt(cg, bg)
                cb4 = jnp.concatenate([cb] * HEADS_PER_GROUP, axis=1)
                arow = jnp.concatenate([acum_t[h:h + 1, rows] for h in heads], axis=1)
                dtrow = jnp.concatenate([dt_t[h:h + 1, rows] for h in heads], axis=1)
                seg = a_x[rows, HEADS_PER_GROUP * Q * g:HEADS_PER_GROUP * Q * (g + 1)] - arow
                mp = cb4 * jnp.exp(jnp.where(causal, seg, -jnp.inf)) * dtrow
                xg = xs_ref[rows, gx]
                xg_b = xg.astype(_BF16)
                bd = jnp.where(blockdiag, jnp.concatenate([xg_b] * HEADS_PER_GROUP, axis=0),
                               jnp.zeros((), _BF16))
                y_diag = _dot(mp.astype(_BF16), bd)
                st = st_ref[s, g]
                y_off = _dot(cg, st.astype(_BF16)) * ea_x[rows, gx]
                y_g = y_diag + y_off + dsk_ref[:, gx] * xg
                v_ref[rows, gx] = y_g
                xw = (xg * w_x[rows, gx]).astype(_BF16)
                st_ref[s, g] = st * ea_x[r0 + Q - 1:r0 + Q, gx] + _dot_tn(bg, xw)
                pacing.done(y_g, CHUNKS_PER_SSD_PIECE)

    pacing.require(n_needed + 1 + len(z_chunks))
    for s, rb, r0 in blocks:
        rows = slice(r0, r0 + ROW_BLOCK)
        ssq_b = jnp.zeros((ROW_BLOCK, 1), _F32)
        for cols in _col_chunks(D_B):
            vb = v_ref[rows, cols] * jax.nn.silu(z_ref[rows, cols])
            v_ref[rows, cols] = vb
            ssq_b = ssq_b + _sumsq(vb)
        scale = lax.rsqrt(ssq_b * (1.0 / D_B) + EPS)
        ycat_ref[rows, D_A:D_A + D_B] = (v_ref[rows, :] * scale * nbw_ref[...]).astype(_BF16)

    out = _dot(ycat_ref[...], wout_ref[...])
    for s, rb, r0 in blocks:
        seq_rows = slice(rb * ROW_BLOCK, (rb + 1) * ROW_BLOCK)
        res = x_ref[s, seq_rows, :] + mod_ref[s, 2:3, :] * out[r0:r0 + ROW_BLOCK]
        y_ref[s, seq_rows, :] = res * lax.rsqrt(_sumsq(res) * (1.0 / D_MODEL) + EPS) * nf_ref[...]

    @pl.when(t == pl.num_programs(1) - 1)
    def _store_carried_state():
        for s in range(S):
            oa_ref[s] = hista_ref[s, SUBLANES - (CONV_A_W - 1):SUBLANES, :]
            ob_ref[s] = histb_ref[s, SUBLANES - (CONV_B_W - 1):SUBLANES, :]
            for g in range(GROUPS):
                os_ref[s, GROUP_W * g:GROUP_W * (g + 1), :] = st_ref[s, g].T


def _mod_kernel(c_ref, w_ref, b_ref, o_ref):
    o_ref[...] = _dot(c_ref[...], w_ref[...]) + b_ref[...]


def _expand_matrix(lanes_per_head):
    e = np.zeros((LANES, HEADS * lanes_per_head), np.float32)
    for k in range(DT_COPIES):
        for h in range(HEADS):
            e[k * HEADS + h, h * lanes_per_head:(h + 1) * lanes_per_head] = 1.0
    return jnp.asarray(e, _BF16)


def _chunk_tri(rows, q):
    i = np.arange(rows)
    return jnp.asarray((i[:, None] // q == i[None, :] // q) & (i[None, :] <= i[:, None]), _BF16)


def _const_spec(shape):
    zeros = (0,) * len(shape)
    return pl.BlockSpec(shape, lambda *_: zeros, pipeline_mode=pl.Buffered(1))


def _layer_call(x, mod3, sa, sb, ss, params, *, S, Lb, Q):
    ns, L, _ = x.shape
    R = S * Lb
    consts = (_expand_matrix(HEADDIM), _expand_matrix(Q), _chunk_tri(R, Q))
    seq3 = lambda i, j: (i, 0, 0)
    state_specs = [pl.BlockSpec((S, CONV_A_W - 1, D_A), seq3),
                   pl.BlockSpec((S, CONV_B_W - 1, D_XBC), seq3),
                   pl.BlockSpec((S, HEADS * HEADDIM, D_STATE), seq3)]
    in_specs = ([pl.BlockSpec((S, Lb, D_MODEL), lambda i, j: (i, j, 0)),
                 pl.BlockSpec((S, 3, D_MODEL), seq3)]
                + state_specs + [_const_spec(a.shape) for a in params + consts])
    out_shape = (jax.ShapeDtypeStruct((ns, L, D_MODEL), _F32),
                 jax.ShapeDtypeStruct((ns, CONV_A_W - 1, D_A), _F32),
                 jax.ShapeDtypeStruct((ns, CONV_B_W - 1, D_XBC), _F32),
                 jax.ShapeDtypeStruct((ns, HEADS * HEADDIM, D_STATE), _F32))
    scratch = [
        pltpu.VMEM((R, D_MODEL), _BF16),
        pltpu.VMEM((R, 4 * D_A), _F32),
        pltpu.VMEM((R, D_B), _F32),
        pltpu.VMEM((R, D_XBC), _F32),
        pltpu.VMEM((R, DT_W), _F32),
        pltpu.VMEM((R, D_A), _F32),
        pltpu.VMEM((R, D_B), _F32),
        pltpu.VMEM((R, D_BC), _BF16),
        pltpu.VMEM((R, D_BC), _BF16),
        pltpu.VMEM((R, D_A + D_B), _BF16),
        pltpu.VMEM((S, SUBLANES, D_A), _F32),
        pltpu.VMEM((S, SUBLANES, D_XBC), _F32),
        pltpu.VMEM((S, GROUPS, D_STATE, GROUP_W), _F32),
    ]
    return pl.pallas_call(
        functools.partial(_layer_kernel, S=S, Lb=Lb, Q=Q),
        grid=(ns // S, L // Lb),
        in_specs=in_specs,
        out_specs=[pl.BlockSpec((S, Lb, D_MODEL), lambda i, j: (i, j, 0))] + state_specs,
        out_shape=out_shape,
        scratch_shapes=scratch,
        compiler_params=pltpu.CompilerParams(
            dimension_semantics=("arbitrary", "arbitrary"),
            vmem_limit_bytes=VMEM_LIMIT_BYTES),
        name=f"layer_s{S}_l{Lb}",
    )(x, mod3, sa, sb, ss, *params, *consts)


def _mod_call(c_all, w_mod, b_mod):
    rows = c_all.shape[0]
    n = w_mod.shape[1]
    bn = D_MODEL
    return pl.pallas_call(
        _mod_kernel,
        grid=(n // bn,),
        in_specs=[pl.BlockSpec((rows, D_MODEL), lambda j: (0, 0)),
                  pl.BlockSpec((D_MODEL, bn), lambda j: (0, j)),
                  pl.BlockSpec((1, bn), lambda j: (0, j))],
        out_specs=pl.BlockSpec((rows, bn), lambda j: (0, j)),
        out_shape=jax.ShapeDtypeStruct((rows, n), _F32),
        name="mod_proj",
    )(c_all, w_mod, b_mod)


def _pad_heads(v):
    return jnp.pad(jnp.tile(v, DT_COPIES), (0, DT_W - DT_COPIES * HEADS)).reshape(1, DT_W)


def kernel(x_prompt, x_sample, state_conv_a, state_conv_b, state_ssm, c_prompt, c_sample, w_mod, b_mod, norm_in_w, w_in, conv_a_w, norm_a_w, conv_b_w, conv_b_b, dt_bias, a_log, d_skip, norm_b_w, w_out, norm_f_w):
    depth = w_in.shape[0]
    assert depth == 1, "single-layer step"
    assert w_in.shape[2] == D_IN_PROJ
    nb, nd = x_prompt.shape[0], x_sample.shape[0]

    n_rows = nb + nd
    rows_pad = -(-n_rows // 16) * 16
    c_all = jnp.pad(jnp.concatenate([c_prompt, c_sample], axis=0), ((0, rows_pad - n_rows), (0, 0)))
    mod = _mod_call(c_all.astype(_BF16), w_mod[0].astype(_BF16), b_mod[0].reshape(1, -1))
    mod3 = mod[:n_rows].reshape(n_rows, 3, D_MODEL)

    w_in_b = w_in[0].astype(_BF16)
    w_dt = w_in_b[:, OFF_DT:]
    w_dt3 = jnp.concatenate([w_dt] * DT_COPIES + [jnp.zeros((D_MODEL, DT_W - DT_COPIES * HEADS), _BF16)], axis=1)
    params = (
        norm_in_w[0].reshape(1, D_MODEL), w_in_b, w_dt3, conv_a_w[0], norm_a_w[0].reshape(1, D_A),
        conv_b_w[0], conv_b_b[0].reshape(1, D_XBC), _pad_heads(dt_bias[0]), _pad_heads(a_log[0]),
        jnp.repeat(d_skip[0], HEADDIM).reshape(1, D_B), norm_b_w[0].reshape(1, D_B),
        w_out[0].astype(_BF16), norm_f_w.reshape(1, D_MODEL),
    )

    f32 = x_prompt.dtype
    za = jnp.zeros((nb, CONV_A_W - 1, D_A), f32)
    zb = jnp.zeros((nb, CONV_B_W - 1, D_XBC), f32)
    zs = jnp.zeros((nb, HEADS * HEADDIM, D_STATE), state_ssm.dtype)
    y_p, ca_p, cb_p, ss_p = _layer_call(x_prompt, mod3[:nb], za, zb, zs, params, S=1, Lb=256, Q=128)
    y_s, ca_s, cb_s, ss_s = _layer_call(
        x_sample, mod3[nb:], state_conv_a[0], state_conv_b[0],
        state_ssm[0].reshape(nd, HEADS * HEADDIM, D_STATE), params, S=4, Lb=x_sample.shape[1], Q=64)

    shp = lambda a: a.reshape(1, a.shape[0], HEADS, HEADDIM, D_STATE)
    return (y_p, y_s, ca_p[None], cb_p[None], shp(ss_p), ca_s[None], cb_s[None], shp(ss_s))
```

```python
import functools

import jax
import jax.numpy as jnp
import numpy as np
from jax import lax
from jax.experimental import pallas as pl
from jax.experimental.pallas import tpu as pltpu

D_MODEL = 1024
D_A = 1024
D_B = 1024
CONV_A_W = 3
CONV_B_W = 4
HEADS = 16
HEADDIM = 64
GROUPS = 4
HEADS_PER_GROUP = HEADS // GROUPS
D_STATE = 128
D_BC = GROUPS * D_STATE
D_XBC = D_B + 2 * D_BC
D_IN_PROJ = 4 * D_A + D_B + D_XBC + HEADS
GROUP_W = HEADS_PER_GROUP * HEADDIM
EPS = 1e-5

LANES = 128
SUBLANES = 8
MXU_W = 256
ROW_BLOCK = 32
DT_COPIES = 3
DT_W = LANES
ROW_PAD = LANES

OFF_A = 0
OFF_Z = 4 * D_A
OFF_XBC = OFF_Z + D_B
OFF_DT = OFF_XBC + D_XBC

VMEM_LIMIT_BYTES = 56 * 1024 * 1024

_F32 = jnp.float32
_BF16 = jnp.bfloat16


def _dot(a, b):
    return jnp.dot(a, b, preferred_element_type=_F32)


def _dot_nt(a, b):
    return lax.dot_general(a, b, (((1,), (1,)), ((), ())), preferred_element_type=_F32)


def _dot_tn(a, b):
    return lax.dot_general(a, b, (((0,), (0,)), ((), ())), preferred_element_type=_F32)


def _sumsq(v):
    return jnp.sum(v * v, axis=-1, keepdims=True)


def _split3(v):
    hi = v.astype(_BF16)
    r1 = v - hi.astype(_F32)
    mid = r1.astype(_BF16)
    lo = (r1 - mid.astype(_F32)).astype(_BF16)
    lane = lax.broadcasted_iota(jnp.int32, v.shape, 1)
    return jnp.where(lane < HEADS, hi, jnp.where(lane < 2 * HEADS, mid, lo))


def _col_chunks(width):
    return [slice(c, c + MXU_W) for c in range(0, width, MXU_W)]


def _row_blocks(S, Lb):
    return [(s, rb, s * Lb + rb * ROW_BLOCK) for s in range(S) for rb in range(Lb // ROW_BLOCK)]


def _causal_conv(val, halo, w):
    n = val.shape[0]
    width = w.shape[0]
    ext = jnp.concatenate([halo, val], axis=0)
    acc = val * w[width - 1:width]
    for k in range(width - 1):
        shifted = pltpu.roll(ext, width - 1 - k, axis=0)[SUBLANES:SUBLANES + n]
        acc = acc + shifted * w[k:k + 1]
    return acc


class _Pacing:
    LAG = 3

    def __init__(self, chunks, hb_ref):
        self.pending = list(chunks)
        self.issued = 0
        self.hb_ref = hb_ref
        self.zeros = []
        self.credit = 0.0

    def _issue(self):
        self.pending.pop(0)()
        self.issued += 1

    def extend(self, chunks):
        self.pending += list(chunks)
        self.credit = 0.0

    def require(self, n):
        while self.issued < n:
            self._issue()

    def gate(self, w):
        if len(self.zeros) < self.LAG:
            return w
        zrow = pltpu.bitcast(self.zeros[-self.LAG], _F32)[0:1, :]
        return w + jnp.concatenate([zrow] * (w.shape[1] // LANES), axis=1)

    def done(self, value, chunks=0.0):
        rows, width = value.shape
        folded = jnp.sum(value.reshape(rows // SUBLANES, SUBLANES, width), axis=0)
        folded = sum(folded[:, c:c + LANES] for c in range(0, width, LANES))
        half = jnp.uint32(16)
        zero = lax.shift_right_logical(
            lax.shift_right_logical(pltpu.bitcast(folded, jnp.uint32), half), half)
        self.zeros.append(zero)
        self.credit += chunks
        if self.pending and self.credit >= 1.0:
            zf = pltpu.bitcast(zero, _F32)
            tile = self.hb_ref[0:2 * SUBLANES, 0:LANES].astype(_F32) + jnp.concatenate([zf, zf], axis=0)
            self.hb_ref[0:2 * SUBLANES, 0:LANES] = tile.astype(_BF16)
            while self.pending and self.credit >= 1.0:
                self.credit -= 1.0
                self._issue()


CHUNKS_PER_CONV_PIECE = 0.3
CHUNKS_PER_SSD_PIECE = 0.5
CHUNKS_AHEAD = 6


def _layer_kernel(x_ref, mod_ref, sa_ref, sb_ref, ss_ref,
                  nin_ref, win_ref, wdt_ref, caw_ref, naw_ref, cbw_ref, cbb_ref, dtb_ref, alog_ref,
                  dsk_ref, nbw_ref, wout_ref, nf_ref, e64_ref, eq_ref, tri_ref,
                  y_ref, oa_ref, ob_ref, os_ref,
                  hb_ref, pa_ref, z_ref, xr_ref, dt_ref, v_ref, xs_ref, bm_ref, cm_ref, ycat_ref,
                  hista_ref, histb_ref, st_ref, *, S, Lb, Q):
    t = pl.program_id(1)
    R = S * Lb
    blocks = _row_blocks(S, Lb)
    last_rb = Lb // ROW_BLOCK - 1
    tail = slice(ROW_BLOCK - SUBLANES, ROW_BLOCK)

    @pl.when(t == 0)
    def _load_carried_state():
        for s in range(S):
            for hist_ref, state_ref, width in ((hista_ref, sa_ref, CONV_A_W), (histb_ref, sb_ref, CONV_B_W)):
                hist_ref[s] = jnp.zeros(hist_ref.shape[1:], _F32)
                hist_ref[s, SUBLANES - (width - 1):SUBLANES, :] = state_ref[s]
            for g in range(GROUPS):
                st_ref[s, g] = ss_ref[s, GROUP_W * g:GROUP_W * (g + 1), :].T

    for s, rb, r0 in blocks:
        x = x_ref[s, rb * ROW_BLOCK:(rb + 1) * ROW_BLOCK, :]
        xn = x * lax.rsqrt(_sumsq(x) * (1.0 / D_MODEL) + EPS) * nin_ref[...]
        hb_ref[r0:r0 + ROW_BLOCK, 0:D_MODEL] = (xn * (1.0 + mod_ref[s, 1:2, :]) + mod_ref[s, 0:1, :]).astype(_BF16)

    def proj(dst_ref, dst_cols, src_off):
        def run():
            dst_ref[:, dst_cols] = _dot(hb_ref[:, 0:D_MODEL], win_ref[:, src_off + dst_cols.start:src_off + dst_cols.stop])
        return run

    def proj_dt():
        dt_ref[...] = _dot(hb_ref[:, 0:D_MODEL], wdt_ref[...])

    pa_chunks = [[proj(pa_ref, slice(k * D_A + cols.start, k * D_A + cols.stop), OFF_A) for k in range(4)]
                 for cols in _col_chunks(D_A)]
    xr_chunks = [proj(xr_ref, cols, OFF_XBC) for cols in _col_chunks(D_XBC)]
    z_chunks = [proj(z_ref, cols, OFF_Z) for cols in _col_chunks(D_B)]

    ssq = {r0: jnp.zeros((ROW_BLOCK, 1), _F32) for _, _, r0 in blocks}

    def conv_a_pieces(cols):
        sh = lambda k: slice(k * D_A + cols.start, k * D_A + cols.stop)
        u_tail = None
        for s, rb, r0 in blocks:
            rows = slice(r0, r0 + ROW_BLOCK)
            u = pa_ref[rows, sh(1)] * pa_ref[rows, sh(2)]
            halo = hista_ref[s, :, cols] if rb == 0 else u_tail
            conv = _causal_conv(u, halo, pacing.gate(caw_ref[:, cols]))
            u_tail = u[tail]
            if rb == last_rb:
                hista_ref[s, :, cols] = u_tail
            va = pa_ref[rows, sh(0)] * conv * jax.nn.silu(pa_ref[rows, sh(3)])
            v_ref[rows, cols] = va
            ssq[r0] = ssq[r0] + _sumsq(va)
            pacing.done(va, CHUNKS_PER_CONV_PIECE)

    def conv_b_pieces(cols):
        for s, rb, r0 in blocks:
            rows = slice(r0, r0 + ROW_BLOCK)
            xr = xr_ref[rows, cols]
            halo = histb_ref[s, :, cols] if rb == 0 else xr_ref[r0 - SUBLANES:r0, cols]
            conv = _causal_conv(xr, halo, pacing.gate(cbw_ref[:, cols]))
            if rb == last_rb:
                histb_ref[s, :, cols] = xr[tail]
            act = jax.nn.silu(conv + cbb_ref[:, cols])
            if cols.start < D_B:
                xs_ref[rows, cols] = act
            elif cols.start < D_B + D_BC:
                bm_ref[rows, cols.start - D_B:cols.stop - D_B] = act.astype(_BF16)
            else:
                cm_ref[rows, cols.start - D_B - D_BC:cols.stop - D_B - D_BC] = act.astype(_BF16)
            pacing.done(act, CHUNKS_PER_CONV_PIECE)

    a_cols, b_cols = _col_chunks(D_A), _col_chunks(D_XBC)
    stages = []
    for j in range(len(a_cols)):
        stages.append((pa_chunks[j], functools.partial(conv_a_pieces, a_cols[j])))
        for jb in (2 * j, 2 * j + 1):
            stages.append(([xr_chunks[jb]], functools.partial(conv_b_pieces, b_cols[jb])))
    pacing = _Pacing([c for needed, _ in stages for c in needed] + [proj_dt], hb_ref)
    pacing.require(CHUNKS_AHEAD)
    n_needed = 0
    for needed, pieces in stages:
        n_needed += len(needed)
        pacing.require(n_needed)
        pieces()
    pacing.require(n_needed + 1)
    pacing.extend(z_chunks)

    for s, rb, r0 in blocks:
        rows = slice(r0, r0 + ROW_BLOCK)
        scale = lax.rsqrt(ssq[r0] * (1.0 / D_A) + EPS)
        ycat_ref[rows, 0:D_A] = (v_ref[rows, 0:D_A] * scale * naw_ref[...]).astype(_BF16)

    dt = jax.nn.softplus(dt_ref[...] + dtb_ref[...])
    da = dt * (-jnp.exp(alog_ref[...]))
    tri = tri_ref[...]
    da_hi = da.astype(_BF16)
    da_r1 = da - da_hi.astype(_F32)
    da_mid = da_r1.astype(_BF16)
    da_lo = (da_r1 - da_mid.astype(_F32)).astype(_BF16)
    acum = _dot(tri, da_hi) + _dot(tri, da_mid) + _dot(tri, da_lo)
    nchunks = R // Q
    alast = jnp.broadcast_to(acum.reshape(nchunks, Q, LANES)[:, Q - 1:Q, :],
                             (nchunks, Q, LANES)).reshape(R, LANES)
    ea_x = _dot(_split3(jnp.exp(acum)), e64_ref[...])
    w_x = _dot(_split3(dt * jnp.exp(alast - acum)), e64_ref[...])
    a_x = _dot(_split3(acum), eq_ref[...])
    acum_t = acum.T
    dt_t = dt.T

    row_i = lax.broadcasted_iota(jnp.int32, (Q, HEADS_PER_GROUP * Q), 0)
    col_i = lax.broadcasted_iota(jnp.int32, (Q, HEADS_PER_GROUP * Q), 1)
    causal = row_i >= (col_i % Q)
    bd_row = lax.broadcasted_iota(jnp.int32, (HEADS_PER_GROUP * Q, GROUP_W), 0) // Q
    bd_col = lax.broadcasted_iota(jnp.int32, (HEADS_PER_GROUP * Q, GROUP_W), 1) // HEADDIM
    blockdiag = bd_row == bd_col

    for s in range(S):
        for c in range(Lb // Q):
            r0 = s * Lb + c * Q
            rows = slice(r0, r0 + Q)
            for g in range(GROUPS):
                gx = slice(GROUP_W * g, GROUP_W * (g + 1))
                gn = slice(D_STATE * g, D_STATE * (g + 1))
                heads = range(HEADS_PER_GROUP * g, HEADS_PER_GROUP * (g + 1))
                cg = cm_ref[rows, gn]
                bg = bm_ref[rows, gn]
                cb = _dot_nt(cg, bg)
                cb4 = jnp.concatenate([cb] * HEADS_PER_GROUP, axis=1)
                arow = jnp.concatenate([acum_t[h:h + 1, rows] for h in heads], axis=1)
                dtrow = jnp.concatenate([dt_t[h:h + 1, rows] for h in heads], axis=1)
                seg = a_x[rows, HEADS_PER_GROUP * Q * g:HEADS_PER_GROUP * Q * (g + 1)] - arow
                mp = cb4 * jnp.exp(jnp.where(causal, seg, -jnp.inf)) * dtrow
                xg = xs_ref[rows, gx]
                xg_b = xg.astype(_BF16)
                bd = jnp.where(blockdiag, jnp.concatenate([xg_b] * HEADS_PER_GROUP, axis=0),
                               jnp.zeros((), _BF16))
                y_diag = _dot(mp.astype(_BF16), bd)
                st = st_ref[s, g]
                y_off = _dot(cg, st.astype(_BF16)) * ea_x[rows, gx]
                y_g = y_diag + y_off + dsk_ref[:, gx] * xg
                v_ref[rows, gx] = y_g
                xw = (xg * w_x[rows, gx]).astype(_BF16)
                st_ref[s, g] = st * ea_x[r0 + Q - 1:r0 + Q, gx] + _dot_tn(bg, xw)
                pacing.done(y_g, CHUNKS_PER_SSD_PIECE)

    pacing.require(n_needed + 1 + len(z_chunks))
    for s, rb, r0 in blocks:
        rows = slice(r0, r0 + ROW_BLOCK)
        ssq_b = jnp.zeros((ROW_BLOCK, 1), _F32)
        for cols in _col_chunks(D_B):
            vb = v_ref[rows, cols] * jax.nn.silu(z_ref[rows, cols])
            v_ref[rows, cols] = vb
            ssq_b = ssq_b + _sumsq(vb)
        scale = lax.rsqrt(ssq_b * (1.0 / D_B) + EPS)
        ycat_ref[rows, D_A:D_A + D_B] = (v_ref[rows, 0:D_B] * scale * nbw_ref[...]).astype(_BF16)

    out = _dot(ycat_ref[:, 0:D_A + D_B], wout_ref[...])
    for s, rb, r0 in blocks:
        seq_rows = slice(rb * ROW_BLOCK, (rb + 1) * ROW_BLOCK)
        res = x_ref[s, seq_rows, :] + mod_ref[s, 2:3, :] * out[r0:r0 + ROW_BLOCK]
        y_ref[s, seq_rows, :] = res * lax.rsqrt(_sumsq(res) * (1.0 / D_MODEL) + EPS) * nf_ref[...]

    @pl.when(t == pl.num_programs(1) - 1)
    def _store_carried_state():
        for s in range(S):
            oa_ref[s] = hista_ref[s, SUBLANES - (CONV_A_W - 1):SUBLANES, :]
            ob_ref[s] = histb_ref[s, SUBLANES - (CONV_B_W - 1):SUBLANES, :]
            for g in range(GROUPS):
                os_ref[s, GROUP_W * g:GROUP_W * (g + 1), :] = st_ref[s, g].T


def _mod_kernel(c_ref, w_ref, b_ref, o_ref):
    o_ref[...] = _dot(c_ref[...], w_ref[...]) + b_ref[...]


def _expand_matrix(lanes_per_head):
    e = np.zeros((LANES, HEADS * lanes_per_head), np.float32)
    for k in range(DT_COPIES):
        for h in range(HEADS):
            e[k * HEADS + h, h * lanes_per_head:(h + 1) * lanes_per_head] = 1.0
    return jnp.asarray(e, _BF16)


def _chunk_tri(rows, q):
    i = np.arange(rows)
    return jnp.asarray((i[:, None] // q == i[None, :] // q) & (i[None, :] <= i[:, None]), _BF16)


def _const_spec(shape):
    zeros = (0,) * len(shape)
    return pl.BlockSpec(shape, lambda *_: zeros, pipeline_mode=pl.Buffered(1))


def _layer_call(x, mod3, sa, sb, ss, params, *, S, Lb, Q):
    ns, L, _ = x.shape
    R = S * Lb
    consts = (_expand_matrix(HEADDIM), _expand_matrix(Q), _chunk_tri(R, Q))
    seq3 = lambda i, j: (i, 0, 0)
    state_specs = [pl.BlockSpec((S, CONV_A_W - 1, D_A), seq3),
                   pl.BlockSpec((S, CONV_B_W - 1, D_XBC), seq3),
                   pl.BlockSpec((S, HEADS * HEADDIM, D_STATE), seq3)]
    in_specs = ([pl.BlockSpec((S, Lb, D_MODEL), lambda i, j: (i, j, 0)),
                 pl.BlockSpec((S, 3, D_MODEL), seq3)]
                + state_specs + [_const_spec(a.shape) for a in params + consts])
    out_shape = (jax.ShapeDtypeStruct((ns, L, D_MODEL), _F32),
                 jax.ShapeDtypeStruct((ns, CONV_A_W - 1, D_A), _F32),
                 jax.ShapeDtypeStruct((ns, CONV_B_W - 1, D_XBC), _F32),
                 jax.ShapeDtypeStruct((ns, HEADS * HEADDIM, D_STATE), _F32))
    scratch = [
        pltpu.VMEM((R, D_MODEL + ROW_PAD), _BF16),
        pltpu.VMEM((R, 4 * D_A + ROW_PAD), _F32),
        pltpu.VMEM((R, D_B + ROW_PAD), _F32),
        pltpu.VMEM((R, D_XBC + ROW_PAD), _F32),
        pltpu.VMEM((R, DT_W), _F32),
        pltpu.VMEM((R, D_A + ROW_PAD), _F32),
        pltpu.VMEM((R, D_B + ROW_PAD), _F32),
        pltpu.VMEM((R, D_BC + ROW_PAD), _BF16),
        pltpu.VMEM((R, D_BC + ROW_PAD), _BF16),
        pltpu.VMEM((R, D_A + D_B + ROW_PAD), _BF16),
        pltpu.VMEM((S, SUBLANES, D_A), _F32),
        pltpu.VMEM((S, SUBLANES, D_XBC), _F32),
        pltpu.VMEM((S, GROUPS, D_STATE, GROUP_W), _F32),
    ]
    return pl.pallas_call(
        functools.partial(_layer_kernel, S=S, Lb=Lb, Q=Q),
        grid=(ns // S, L // Lb),
        in_specs=in_specs,
        out_specs=[pl.BlockSpec((S, Lb, D_MODEL), lambda i, j: (i, j, 0))] + state_specs,
        out_shape=out_shape,
        scratch_shapes=scratch,
        compiler_params=pltpu.CompilerParams(
            dimension_semantics=("arbitrary", "arbitrary"),
            vmem_limit_bytes=VMEM_LIMIT_BYTES),
        name=f"layer_s{S}_l{Lb}",
    )(x, mod3, sa, sb, ss, *params, *consts)


def _mod_call(c_all, w_mod, b_mod):
    rows = c_all.shape[0]
    n = w_mod.shape[1]
    bn = D_MODEL
    return pl.pallas_call(
        _mod_kernel,
        grid=(n // bn,),
        in_specs=[pl.BlockSpec((rows, D_MODEL), lambda j: (0, 0)),
                  pl.BlockSpec((D_MODEL, bn), lambda j: (0, j)),
                  pl.BlockSpec((1, bn), lambda j: (0, j))],
        out_specs=pl.BlockSpec((rows, bn), lambda j: (0, j)),
        out_shape=jax.ShapeDtypeStruct((rows, n), _F32),
        name="mod_proj",
    )(c_all, w_mod, b_mod)


def _pad_heads(v):
    return jnp.pad(jnp.tile(v, DT_COPIES), (0, DT_W - DT_COPIES * HEADS)).reshape(1, DT_W)


def kernel(x_prompt, x_sample, state_conv_a, state_conv_b, state_ssm, c_prompt, c_sample, w_mod, b_mod, norm_in_w, w_in, conv_a_w, norm_a_w, conv_b_w, conv_b_b, dt_bias, a_log, d_skip, norm_b_w, w_out, norm_f_w):
    depth = w_in.shape[0]
    assert depth == 1, "single-layer step"
    assert w_in.shape[2] == D_IN_PROJ
    nb, nd = x_prompt.shape[0], x_sample.shape[0]

    n_rows = nb + nd
    rows_pad = -(-n_rows // 16) * 16
    c_all = jnp.pad(jnp.concatenate([c_prompt, c_sample], axis=0), ((0, rows_pad - n_rows), (0, 0)))
    mod = _mod_call(c_all.astype(_BF16), w_mod[0].astype(_BF16), b_mod[0].reshape(1, -1))
    mod3 = mod[:n_rows].reshape(n_rows, 3, D_MODEL)

    w_in_b = w_in[0].astype(_BF16)
    w_dt = w_in_b[:, OFF_DT:]
    w_dt3 = jnp.concatenate([w_dt] * DT_COPIES + [jnp.zeros((D_MODEL, DT_W - DT_COPIES * HEADS), _BF16)], axis=1)
    params = (
        norm_in_w[0].reshape(1, D_MODEL), w_in_b, w_dt3, conv_a_w[0], norm_a_w[0].reshape(1, D_A),
        conv_b_w[0], conv_b_b[0].reshape(1, D_XBC), _pad_heads(dt_bias[0]), _pad_heads(a_log[0]),
        jnp.repeat(d_skip[0], HEADDIM).reshape(1, D_B), norm_b_w[0].reshape(1, D_B),
        w_out[0].astype(_BF16), norm_f_w.reshape(1, D_MODEL),
    )

    f32 = x_prompt.dtype
    za = jnp.zeros((nb, CONV_A_W - 1, D_A), f32)
    zb = jnp.zeros((nb, CONV_B_W - 1, D_XBC), f32)
    zs = jnp.zeros((nb, HEADS * HEADDIM, D_STATE), state_ssm.dtype)
    y_p, ca_p, cb_p, ss_p = _layer_call(x_prompt, mod3[:nb], za, zb, zs, params, S=1, Lb=256, Q=128)
    y_s, ca_s, cb_s, ss_s = _layer_call(
        x_sample, mod3[nb:], state_conv_a[0], state_conv_b[0],
        state_ssm[0].reshape(nd, HEADS * HEADDIM, D_STATE), params, S=4, Lb=x_sample.shape[1], Q=64)

    shp = lambda a: a.reshape(1, a.shape[0], HEADS, HEADDIM, D_STATE)
    return (y_p, y_s, ca_p[None], cb_p[None], shp(ss_p), ca_s[None], cb_s[None], shp(ss_s))
```

```python
import functools

import jax
import jax.numpy as jnp
import numpy as np
from jax import lax
from jax.experimental import pallas as pl
from jax.experimental.pallas import tpu as pltpu

D_MODEL = 1024
D_A = 1024
D_B = 1024
CONV_A_W = 3
CONV_B_W = 4
HEADS = 16
HEADDIM = 64
GROUPS = 4
HEADS_PER_GROUP = HEADS // GROUPS
D_STATE = 128
D_BC = GROUPS * D_STATE
D_XBC = D_B + 2 * D_BC
D_IN_PROJ = 4 * D_A + D_B + D_XBC + HEADS
GROUP_W = HEADS_PER_GROUP * HEADDIM
EPS = 1e-5

LANES = 128
HIST = 8
DT_COPIES = 3
DT_W = LANES

OFF_A = 0
OFF_Z = 4 * D_A
OFF_XBC = OFF_Z + D_B
OFF_DT = OFF_XBC + D_XBC

VMEM_LIMIT_BYTES = 56 * 1024 * 1024

_F32 = jnp.float32
_BF16 = jnp.bfloat16


def _dot(a, b):
    return jnp.dot(a, b, preferred_element_type=_F32)


def _dot_nt(a, b):
    return lax.dot_general(a, b, (((1,), (1,)), ((), ())), preferred_element_type=_F32)


def _dot_tn(a, b):
    return lax.dot_general(a, b, (((0,), (0,)), ((), ())), preferred_element_type=_F32)


def _rms_scale(v):
    return lax.rsqrt(jnp.mean(v * v, axis=-1, keepdims=True) + EPS)


def _split3(v):
    hi = v.astype(_BF16)
    r1 = v - hi.astype(_F32)
    mid = r1.astype(_BF16)
    lo = (r1 - mid.astype(_F32)).astype(_BF16)
    lane = lax.broadcasted_iota(jnp.int32, v.shape, 1)
    return jnp.where(lane < HEADS, hi, jnp.where(lane < 2 * HEADS, mid, lo))


def _mod_kernel(c_ref, w_ref, b_ref, o_ref):
    o_ref[...] = _dot(c_ref[...], w_ref[...]) + b_ref[...]


def _layer_kernel(x_ref, mod_ref, sa_ref, sb_ref, ss_ref,
                  nin_ref, win_ref, wdt_ref, caw_ref, naw_ref, cbw_ref, cbb_ref, dtb_ref, alog_ref,
                  dsk_ref, nbw_ref, wout_ref, nf_ref, e64_ref, eq_ref, tri_ref,
                  y_ref, oa_ref, ob_ref, os_ref,
                  ubuf, xbuf, st_ref, *, S, Lb, Q):
    t = pl.program_id(1)
    R = S * Lb

    @pl.when(t == 0)
    def _load_carried_state():
        for s in range(S):
            ubuf[s, HIST - (CONV_A_W - 1):HIST, :] = sa_ref[s]
            xbuf[s, HIST - (CONV_B_W - 1):HIST, :] = sb_ref[s]
            for g in range(GROUPS):
                st_ref[s, g] = ss_ref[s, GROUP_W * g:GROUP_W * (g + 1), :].T

    def per_seq(fn):
        parts = [fn(s, slice(s * Lb, (s + 1) * Lb)) for s in range(S)]
        return parts[0] if S == 1 else jnp.concatenate(parts, axis=0)

    x = x_ref[...].reshape(R, D_MODEL)

    xn = x * _rms_scale(x) * nin_ref[...]
    hb = per_seq(lambda s, rows: xn[rows] * (1.0 + mod_ref[s, 1:2, :]) + mod_ref[s, 0:1, :]).astype(_BF16)

    pa = _dot(hb, win_ref[:, OFF_A:OFF_A + 4 * D_A])
    b_gate = pa[:, 0:D_A]
    u = pa[:, D_A:2 * D_A] * pa[:, 2 * D_A:3 * D_A]
    g_a = pa[:, 3 * D_A:4 * D_A]

    def conv_a(s, rows):
        ubuf[s, HIST:HIST + Lb, :] = u[rows]
        acc = u[rows] * caw_ref[CONV_A_W - 1:CONV_A_W, :]
        for k in range(CONV_A_W - 1):
            off = HIST - (CONV_A_W - 1) + k
            acc = acc + ubuf[s, off:off + Lb, :] * caw_ref[k:k + 1, :]
        ubuf[s, HIST - (CONV_A_W - 1):HIST, :] = ubuf[s, HIST + Lb - (CONV_A_W - 1):HIST + Lb, :]
        return acc

    va = b_gate * per_seq(conv_a) * jax.nn.silu(g_a)
    ya = (va * _rms_scale(va) * naw_ref[...]).astype(_BF16)

    z = _dot(hb, win_ref[:, OFF_Z:OFF_Z + D_B])
    xr = _dot(hb, win_ref[:, OFF_XBC:OFF_XBC + D_XBC])
    dtr = _dot(hb, wdt_ref[...])

    def conv_b(s, rows):
        xbuf[s, HIST:HIST + Lb, :] = xr[rows]
        acc = xr[rows] * cbw_ref[CONV_B_W - 1:CONV_B_W, :]
        for k in range(CONV_B_W - 1):
            off = HIST - (CONV_B_W - 1) + k
            acc = acc + xbuf[s, off:off + Lb, :] * cbw_ref[k:k + 1, :]
        xbuf[s, HIST - (CONV_B_W - 1):HIST, :] = xbuf[s, HIST + Lb - (CONV_B_W - 1):HIST + Lb, :]
        return acc

    xbc = jax.nn.silu(per_seq(conv_b) + cbb_ref[...])
    xs = xbc[:, 0:D_B]
    bm = xbc[:, D_B:D_B + D_BC].astype(_BF16)
    cm = xbc[:, D_B + D_BC:D_XBC].astype(_BF16)

    dt = jax.nn.softplus(dtr + dtb_ref[...])
    da = dt * (-jnp.exp(alog_ref[...]))
    tri = tri_ref[...]
    da_hi = da.astype(_BF16)
    da_r1 = da - da_hi.astype(_F32)
    da_mid = da_r1.astype(_BF16)
    da_lo = (da_r1 - da_mid.astype(_F32)).astype(_BF16)
    acum = _dot(tri, da_hi) + _dot(tri, da_mid) + _dot(tri, da_lo)
    nchunks = R // Q
    alast = jnp.broadcast_to(acum.reshape(nchunks, Q, LANES)[:, Q - 1:Q, :],
                             (nchunks, Q, LANES)).reshape(R, LANES)
    ea_x = _dot(_split3(jnp.exp(acum)), e64_ref[...])
    w_x = _dot(_split3(dt * jnp.exp(alast - acum)), e64_ref[...])
    a_x = _dot(_split3(acum), eq_ref[...])
    acum_t = acum.T
    dt_t = dt.T

    row_i = lax.broadcasted_iota(jnp.int32, (Q, HEADS_PER_GROUP * Q), 0)
    col_i = lax.broadcasted_iota(jnp.int32, (Q, HEADS_PER_GROUP * Q), 1)
    causal = row_i >= (col_i % Q)
    bd_row = lax.broadcasted_iota(jnp.int32, (HEADS_PER_GROUP * Q, GROUP_W), 0) // Q
    bd_col = lax.broadcasted_iota(jnp.int32, (HEADS_PER_GROUP * Q, GROUP_W), 1) // HEADDIM
    blockdiag = bd_row == bd_col

    y_rows = []
    for s in range(S):
        for c in range(Lb // Q):
            r0 = s * Lb + c * Q
            rows = slice(r0, r0 + Q)
            y_groups = []
            for g in range(GROUPS):
                gx = slice(GROUP_W * g, GROUP_W * (g + 1))
                gn = slice(D_STATE * g, D_STATE * (g + 1))
                heads = range(HEADS_PER_GROUP * g, HEADS_PER_GROUP * (g + 1))
                cg = cm[rows, gn]
                bg = bm[rows, gn]
                cb = _dot_nt(cg, bg)
                cb4 = jnp.concatenate([cb] * HEADS_PER_GROUP, axis=1)
                arow = jnp.concatenate([acum_t[h:h + 1, rows] for h in heads], axis=1)
                dtrow = jnp.concatenate([dt_t[h:h + 1, rows] for h in heads], axis=1)
                seg = a_x[rows, HEADS_PER_GROUP * Q * g:HEADS_PER_GROUP * Q * (g + 1)] - arow
                mp = cb4 * jnp.exp(jnp.where(causal, seg, -jnp.inf)) * dtrow
                xg = xs[rows, gx]
                xg_b = xg.astype(_BF16)
                bd = jnp.where(blockdiag, jnp.concatenate([xg_b] * HEADS_PER_GROUP, axis=0),
                               jnp.zeros((), _BF16))
                y_diag = _dot(mp.astype(_BF16), bd)
                st = st_ref[s, g]
                y_off = _dot(cg, st.astype(_BF16)) * ea_x[rows, gx]
                y_groups.append(y_diag + y_off)
                xw = (xg * w_x[rows, gx]).astype(_BF16)
                st_ref[s, g] = st * ea_x[r0 + Q - 1:r0 + Q, gx] + _dot_tn(bg, xw)
            y_rows.append(jnp.concatenate(y_groups, axis=1))
    y = y_rows[0] if len(y_rows) == 1 else jnp.concatenate(y_rows, axis=0)

    y = y + dsk_ref[...] * xs
    vb = y * jax.nn.silu(z)
    yb = (vb * _rms_scale(vb) * nbw_ref[...]).astype(_BF16)

    out = _dot(ya, wout_ref[0:D_A, :]) + _dot(yb, wout_ref[D_A:D_A + D_B, :])
    res = per_seq(lambda s, rows: x[rows] + mod_ref[s, 2:3, :] * out[rows])
    y_ref[...] = (res * _rms_scale(res) * nf_ref[...]).reshape(S, Lb, D_MODEL)

    @pl.when(t == pl.num_programs(1) - 1)
    def _store_carried_state():
        for s in range(S):
            oa_ref[s] = ubuf[s, HIST - (CONV_A_W - 1):HIST, :]
            ob_ref[s] = xbuf[s, HIST - (CONV_B_W - 1):HIST, :]
            for g in range(GROUPS):
                os_ref[s, GROUP_W * g:GROUP_W * (g + 1), :] = st_ref[s, g].T


def _expand_matrix(lanes_per_head):
    e = np.zeros((LANES, HEADS * lanes_per_head), np.float32)
    for k in range(DT_COPIES):
        for h in range(HEADS):
            e[k * HEADS + h, h * lanes_per_head:(h + 1) * lanes_per_head] = 1.0
    return jnp.asarray(e, _BF16)


def _chunk_tri(rows, q):
    i = np.arange(rows)
    return jnp.asarray((i[:, None] // q == i[None, :] // q) & (i[None, :] <= i[:, None]), _BF16)


def _const_spec(shape):
    return pl.BlockSpec(shape, lambda i, j: (0,) * len(shape), pipeline_mode=pl.Buffered(1))


def _layer_call(x, mod3, sa, sb, ss, params, *, S, Lb, Q):
    ns, L, _ = x.shape
    R = S * Lb
    consts = (_expand_matrix(HEADDIM), _expand_matrix(Q), _chunk_tri(R, Q))
    seq3 = lambda i, j: (i, 0, 0)
    in_specs = [
        pl.BlockSpec((S, Lb, D_MODEL), lambda i, j: (i, j, 0)),
        pl.BlockSpec((S, 3, D_MODEL), seq3),
        pl.BlockSpec((S, CONV_A_W - 1, D_A), seq3),
        pl.BlockSpec((S, CONV_B_W - 1, D_XBC), seq3),
        pl.BlockSpec((S, HEADS * HEADDIM, D_STATE), seq3),
    ] + [_const_spec(p.shape) for p in params + consts]
    out_shape = (
        jax.ShapeDtypeStruct((ns, L, D_MODEL), _F32),
        jax.ShapeDtypeStruct((ns, CONV_A_W - 1, D_A), _F32),
        jax.ShapeDtypeStruct((ns, CONV_B_W - 1, D_XBC), _F32),
        jax.ShapeDtypeStruct((ns, HEADS * HEADDIM, D_STATE), _F32),
    )
    out_specs = (
        pl.BlockSpec((S, Lb, D_MODEL), lambda i, j: (i, j, 0)),
        pl.BlockSpec((S, CONV_A_W - 1, D_A), seq3),
        pl.BlockSpec((S, CONV_B_W - 1, D_XBC), seq3),
        pl.BlockSpec((S, HEADS * HEADDIM, D_STATE), seq3),
    )
    scratch = [
        pltpu.VMEM((S, HIST + Lb, D_A), _F32),
        pltpu.VMEM((S, HIST + Lb, D_XBC), _F32),
        pltpu.VMEM((S, GROUPS, D_STATE, GROUP_W), _F32),
    ]
    return pl.pallas_call(
        functools.partial(_layer_kernel, S=S, Lb=Lb, Q=Q),
        grid=(ns // S, L // Lb),
        in_specs=in_specs,
        out_specs=out_specs,
        out_shape=out_shape,
        scratch_shapes=scratch,
        compiler_params=pltpu.CompilerParams(
            dimension_semantics=("arbitrary", "arbitrary"),
            vmem_limit_bytes=VMEM_LIMIT_BYTES),
        name=f"layer_s{S}_l{Lb}",
    )(x, mod3, sa, sb, ss, *params, *consts)


def _mod_call(c_all, w_mod, b_mod):
    rows = c_all.shape[0]
    n = w_mod.shape[1]
    bn = D_MODEL
    return pl.pallas_call(
        _mod_kernel,
        grid=(n // bn,),
        in_specs=[pl.BlockSpec((rows, D_MODEL), lambda j: (0, 0)),
                  pl.BlockSpec((D_MODEL, bn), lambda j: (0, j)),
                  pl.BlockSpec((1, bn), lambda j: (0, j))],
        out_specs=pl.BlockSpec((rows, bn), lambda j: (0, j)),
        out_shape=jax.ShapeDtypeStruct((rows, n), _F32),
        name="mod_proj",
    )(c_all, w_mod, b_mod)


def _pad_heads(v):
    return jnp.pad(jnp.tile(v, DT_COPIES), (0, DT_W - DT_COPIES * HEADS)).reshape(1, DT_W)


def kernel(x_prompt, x_sample, state_conv_a, state_conv_b, state_ssm, c_prompt, c_sample, w_mod, b_mod, norm_in_w, w_in, conv_a_w, norm_a_w, conv_b_w, conv_b_b, dt_bias, a_log, d_skip, norm_b_w, w_out, norm_f_w):
    depth = w_in.shape[0]
    assert depth == 1, "single-layer step"
    assert w_in.shape[2] == D_IN_PROJ
    nb, nd = x_prompt.shape[0], x_sample.shape[0]

    n_rows = nb + nd
    rows_pad = -(-n_rows // 16) * 16
    c_all = jnp.pad(jnp.concatenate([c_prompt, c_sample], axis=0), ((0, rows_pad - n_rows), (0, 0)))
    mod = _mod_call(c_all.astype(_BF16), w_mod[0].astype(_BF16), b_mod[0].reshape(1, -1))
    mod3 = mod[:n_rows].reshape(n_rows, 3, D_MODEL)

    w_in_b = w_in[0].astype(_BF16)
    w_dt = w_in_b[:, OFF_DT:]
    w_dt3 = jnp.concatenate([w_dt] * DT_COPIES + [jnp.zeros((D_MODEL, DT_W - DT_COPIES * HEADS), _BF16)], axis=1)
    params = (
        norm_in_w[0].reshape(1, D_MODEL), w_in_b, w_dt3, conv_a_w[0], norm_a_w[0].reshape(1, D_A),
        conv_b_w[0], conv_b_b[0].reshape(1, D_XBC), _pad_heads(dt_bias[0]), _pad_heads(a_log[0]),
        jnp.repeat(d_skip[0], HEADDIM).reshape(1, D_B), norm_b_w[0].reshape(1, D_B),
        w_out[0].astype(_BF16), norm_f_w.reshape(1, D_MODEL),
    )

    f32 = x_prompt.dtype
    za = jnp.zeros((nb, CONV_A_W - 1, D_A), f32)
    zb = jnp.zeros((nb, CONV_B_W - 1, D_XBC), f32)
    zs = jnp.zeros((nb, HEADS * HEADDIM, D_STATE), state_ssm.dtype)
    y_p, ca_p, cb_p, ss_p = _layer_call(x_prompt, mod3[:nb], za, zb, zs, params, S=1, Lb=256, Q=128)
    y_s, ca_s, cb_s, ss_s = _layer_call(
        x_sample, mod3[nb:], state_conv_a[0], state_conv_b[0],
        state_ssm[0].reshape(nd, HEADS * HEADDIM, D_STATE), params, S=4, Lb=x_sample.shape[1], Q=64)

    shp = lambda a: a.reshape(1, a.shape[0], HEADS, HEADDIM, D_STATE)
    return (y_p, y_s, ca_p[None], cb_p[None], shp(ss_p), ca_s[None], cb_s[None], shp(ss_s))
```

```python
import functools

import jax
import jax.numpy as jnp
import numpy as np
from jax import lax
from jax.experimental import pallas as pl
from jax.experimental.pallas import tpu as pltpu

D_MODEL = 1024
D_A = 1024
D_B = 1024
CONV_A_W = 3
CONV_B_W = 4
HEADS = 16
HEADDIM = 64
GROUPS = 4
HEADS_PER_GROUP = HEADS // GROUPS
D_STATE = 128
D_BC = GROUPS * D_STATE
D_XBC = D_B + 2 * D_BC
D_IN_PROJ = 4 * D_A + D_B + D_XBC + HEADS
GROUP_W = HEADS_PER_GROUP * HEADDIM
EPS = 1e-5
LOG2_E = 1.4426950408889634

LANES = 128
HIST = 8
DT_COPIES = 3
DT_W = LANES

OFF_A = 0
OFF_Z = 4 * D_A
OFF_XBC = OFF_Z + D_B
OFF_DT = OFF_XBC + D_XBC

VMEM_LIMIT_BYTES = 56 * 1024 * 1024

_F32 = jnp.float32
_BF16 = jnp.bfloat16


def _dot(a, b):
    return jnp.dot(a, b, preferred_element_type=_F32)


def _dot_nt(a, b):
    return lax.dot_general(a, b, (((1,), (1,)), ((), ())), preferred_element_type=_F32)


def _dot_tn(a, b):
    return lax.dot_general(a, b, (((0,), (0,)), ((), ())), preferred_element_type=_F32)


def _rms_scale(v):
    return lax.rsqrt(jnp.mean(v * v, axis=-1, keepdims=True) + EPS)


def _split3(v):
    hi = v.astype(_BF16)
    r1 = v - hi.astype(_F32)
    mid = r1.astype(_BF16)
    lo = (r1 - mid.astype(_F32)).astype(_BF16)
    lane = lax.broadcasted_iota(jnp.int32, v.shape, 1)
    return jnp.where(lane < HEADS, hi, jnp.where(lane < 2 * HEADS, mid, lo))


def _mod_kernel(c_ref, w_ref, b_ref, o_ref):
    o_ref[...] = _dot(c_ref[...], w_ref[...]) + b_ref[...]


def _layer_kernel(x_ref, mod_ref, sa_ref, sb_ref, ss_ref,
                  nin_ref, win_ref, wdt_ref, caw_ref, naw_ref, cbw_ref, cbb_ref, dtb_ref, alog_ref,
                  dsk_ref, nbw_ref, wout_ref, nf_ref, e64_ref, eq_ref, tri_ref,
                  y_ref, oa_ref, ob_ref, os_ref,
                  ubuf, xbuf, st_ref, *, S, Lb, Q):
    t = pl.program_id(1)
    R = S * Lb

    @pl.when(t == 0)
    def _load_carried_state():
        for s in range(S):
            for hist_ref, state_ref, width in ((ubuf, sa_ref, CONV_A_W), (xbuf, sb_ref, CONV_B_W)):
                hist_ref[s] = jnp.zeros(hist_ref.shape[1:], _F32)
                hist_ref[s, HIST - (width - 1):HIST, :] = state_ref[s]
            for g in range(GROUPS):
                st_ref[s, g] = ss_ref[s, GROUP_W * g:GROUP_W * (g + 1), :].T

    def per_seq(fn):
        parts = [fn(s, slice(s * Lb, (s + 1) * Lb)) for s in range(S)]
        return parts[0] if S == 1 else jnp.concatenate(parts, axis=0)

    x = x_ref[...].reshape(R, D_MODEL)

    xn = x * _rms_scale(x)
    hb = per_seq(lambda s, rows: xn[rows] * (nin_ref[...] * (1.0 + mod_ref[s, 1:2, :])) + mod_ref[s, 0:1, :]).astype(_BF16)

    pa = _dot(hb, win_ref[:, OFF_A:OFF_A + 4 * D_A])
    b_gate = pa[:, 0:D_A]
    u = pa[:, D_A:2 * D_A] * pa[:, 2 * D_A:3 * D_A]
    g_a = pa[:, 3 * D_A:4 * D_A]

    def causal_conv(val, hist_ref, s, w_ref):
        width = w_ref.shape[0]
        ext = jnp.concatenate([hist_ref[s], val], axis=0)
        acc = val * w_ref[width - 1:width, :]
        for d in range(1, width):
            acc = acc + pltpu.roll(ext, d, axis=0)[HIST:HIST + Lb] * w_ref[width - 1 - d:width - d, :]
        hist_ref[s] = val[Lb - HIST:Lb]
        return acc

    va = b_gate * per_seq(lambda s, rows: causal_conv(u[rows], ubuf, s, caw_ref)) * jax.nn.silu(g_a)
    ya = (va * _rms_scale(va) * naw_ref[...]).astype(_BF16)

    z = _dot(hb, win_ref[:, OFF_Z:OFF_Z + D_B])
    xr = _dot(hb, win_ref[:, OFF_XBC:OFF_XBC + D_XBC])
    dtr = _dot(hb, wdt_ref[...])

    xbc = jax.nn.silu(per_seq(lambda s, rows: causal_conv(xr[rows], xbuf, s, cbw_ref)) + cbb_ref[...])
    xs = xbc[:, 0:D_B]
    bm = xbc[:, D_B:D_B + D_BC].astype(_BF16)
    cm = xbc[:, D_B + D_BC:D_XBC].astype(_BF16)

    dt = jax.nn.softplus(dtr + dtb_ref[...])
    da = dt * (-jnp.exp(alog_ref[...]))
    tri = tri_ref[...]
    da_hi = da.astype(_BF16)
    da_r1 = da - da_hi.astype(_F32)
    da_mid = da_r1.astype(_BF16)
    da_lo = (da_r1 - da_mid.astype(_F32)).astype(_BF16)
    acum = _dot(tri, da_hi) + _dot(tri, da_mid) + _dot(tri, da_lo)
    nchunks = R // Q
    alast = jnp.broadcast_to(acum.reshape(nchunks, Q, LANES)[:, Q - 1:Q, :],
                             (nchunks, Q, LANES)).reshape(R, LANES)
    ea_x = _dot(_split3(jnp.exp(acum)), e64_ref[...])
    w_x = _dot(_split3(dt * jnp.exp(alast - acum)), e64_ref[...])
    acum2 = acum * LOG2_E
    a_x = _dot(_split3(acum2), eq_ref[...])
    acum_t = acum2.T
    dt_t = dt.T

    row_i = lax.broadcasted_iota(jnp.int32, (Q, HEADS_PER_GROUP * Q), 0)
    col_i = lax.broadcasted_iota(jnp.int32, (Q, HEADS_PER_GROUP * Q), 1)
    causal = row_i >= (col_i % Q)
    bd_row = lax.broadcasted_iota(jnp.int32, (HEADS_PER_GROUP * Q, GROUP_W), 0) // Q
    bd_col = lax.broadcasted_iota(jnp.int32, (HEADS_PER_GROUP * Q, GROUP_W), 1) // HEADDIM
    blockdiag = bd_row == bd_col

    y_rows = []
    for s in range(S):
        for c in range(Lb // Q):
            r0 = s * Lb + c * Q
            rows = slice(r0, r0 + Q)
            y_groups = []
            for g in range(GROUPS):
                gx = slice(GROUP_W * g, GROUP_W * (g + 1))
                gn = slice(D_STATE * g, D_STATE * (g + 1))
                heads = range(HEADS_PER_GROUP * g, HEADS_PER_GROUP * (g + 1))
                cg = cm[rows, gn]
                bg = bm[rows, gn]
                cb = _dot_nt(cg, bg)
                cb4 = jnp.concatenate([cb] * HEADS_PER_GROUP, axis=1)
                arow = jnp.concatenate([acum_t[h:h + 1, rows] for h in heads], axis=1)
                dtrow = jnp.concatenate([dt_t[h:h + 1, rows] for h in heads], axis=1)
                seg = a_x[rows, HEADS_PER_GROUP * Q * g:HEADS_PER_GROUP * Q * (g + 1)] - arow
                mp = cb4 * jnp.exp2(jnp.where(causal, seg, -jnp.inf)) * dtrow
                xg = xs[rows, gx]
                xg_b = xg.astype(_BF16)
                bd = jnp.where(blockdiag, jnp.concatenate([xg_b] * HEADS_PER_GROUP, axis=0),
                               jnp.zeros((), _BF16))
                y_diag = _dot(mp.astype(_BF16), bd)
                st = st_ref[s, g]
                y_off = _dot(cg, st.astype(_BF16)) * ea_x[rows, gx]
                y_groups.append(y_diag + y_off)
                xw = (xg * w_x[rows, gx]).astype(_BF16)
                st_ref[s, g] = st * ea_x[r0 + Q - 1:r0 + Q, gx] + _dot_tn(bg, xw)
            y_rows.append(jnp.concatenate(y_groups, axis=1))
    y = y_rows[0] if len(y_rows) == 1 else jnp.concatenate(y_rows, axis=0)

    y = y + dsk_ref[...] * xs
    vb = y * jax.nn.silu(z)
    yb = (vb * _rms_scale(vb) * nbw_ref[...]).astype(_BF16)

    out = _dot(ya, wout_ref[0:D_A, :]) + _dot(yb, wout_ref[D_A:D_A + D_B, :])
    res = per_seq(lambda s, rows: x[rows] + mod_ref[s, 2:3, :] * out[rows])
    y_ref[...] = (res * _rms_scale(res) * nf_ref[...]).reshape(S, Lb, D_MODEL)

    @pl.when(t == pl.num_programs(1) - 1)
    def _store_carried_state():
        for s in range(S):
            oa_ref[s] = ubuf[s, HIST - (CONV_A_W - 1):HIST, :]
            ob_ref[s] = xbuf[s, HIST - (CONV_B_W - 1):HIST, :]
            for g in range(GROUPS):
                os_ref[s, GROUP_W * g:GROUP_W * (g + 1), :] = st_ref[s, g].T


def _expand_matrix(lanes_per_head):
    e = np.zeros((LANES, HEADS * lanes_per_head), np.float32)
    for k in range(DT_COPIES):
        for h in range(HEADS):
            e[k * HEADS + h, h * lanes_per_head:(h + 1) * lanes_per_head] = 1.0
    return jnp.asarray(e, _BF16)


def _chunk_tri(rows, q):
    i = np.arange(rows)
    return jnp.asarray((i[:, None] // q == i[None, :] // q) & (i[None, :] <= i[:, None]), _BF16)


def _const_spec(shape):
    return pl.BlockSpec(shape, lambda i, j: (0,) * len(shape), pipeline_mode=pl.Buffered(1))


def _layer_call(x, mod3, sa, sb, ss, params, *, S, Lb, Q):
    ns, L, _ = x.shape
    R = S * Lb
    consts = (_expand_matrix(HEADDIM), _expand_matrix(Q), _chunk_tri(R, Q))
    seq3 = lambda i, j: (i, 0, 0)
    in_specs = [
        pl.BlockSpec((S, Lb, D_MODEL), lambda i, j: (i, j, 0)),
        pl.BlockSpec((S, 3, D_MODEL), seq3),
        pl.BlockSpec((S, CONV_A_W - 1, D_A), seq3),
        pl.BlockSpec((S, CONV_B_W - 1, D_XBC), seq3),
        pl.BlockSpec((S, HEADS * HEADDIM, D_STATE), seq3),
    ] + [_const_spec(p.shape) for p in params + consts]
    out_shape = (
        jax.ShapeDtypeStruct((ns, L, D_MODEL), _F32),
        jax.ShapeDtypeStruct((ns, CONV_A_W - 1, D_A), _F32),
        jax.ShapeDtypeStruct((ns, CONV_B_W - 1, D_XBC), _F32),
        jax.ShapeDtypeStruct((ns, HEADS * HEADDIM, D_STATE), _F32),
    )
    out_specs = (
        pl.BlockSpec((S, Lb, D_MODEL), lambda i, j: (i, j, 0)),
        pl.BlockSpec((S, CONV_A_W - 1, D_A), seq3),
        pl.BlockSpec((S, CONV_B_W - 1, D_XBC), seq3),
        pl.BlockSpec((S, HEADS * HEADDIM, D_STATE), seq3),
    )
    scratch = [
        pltpu.VMEM((S, HIST, D_A), _F32),
        pltpu.VMEM((S, HIST, D_XBC), _F32),
        pltpu.VMEM((S, GROUPS, D_STATE, GROUP_W), _F32),
    ]
    return pl.pallas_call(
        functools.partial(_layer_kernel, S=S, Lb=Lb, Q=Q),
        grid=(ns // S, L // Lb),
        in_specs=in_specs,
        out_specs=out_specs,
        out_shape=out_shape,
        scratch_shapes=scratch,
        compiler_params=pltpu.CompilerParams(
            dimension_semantics=("arbitrary", "arbitrary"),
            vmem_limit_bytes=VMEM_LIMIT_BYTES),
        name=f"layer_s{S}_l{Lb}",
    )(x, mod3, sa, sb, ss, *params, *consts)


def _mod_call(c_all, w_mod, b_mod):
    rows = c_all.shape[0]
    n = w_mod.shape[1]
    bn = D_MODEL
    return pl.pallas_call(
        _mod_kernel,
        grid=(n // bn,),
        in_specs=[pl.BlockSpec((rows, D_MODEL), lambda j: (0, 0)),
                  pl.BlockSpec((D_MODEL, bn), lambda j: (0, j)),
                  pl.BlockSpec((1, bn), lambda j: (0, j))],
        out_specs=pl.BlockSpec((rows, bn), lambda j: (0, j)),
        out_shape=jax.ShapeDtypeStruct((rows, n), _F32),
        name="mod_proj",
    )(c_all, w_mod, b_mod)


def _pad_heads(v):
    return jnp.pad(jnp.tile(v, DT_COPIES), (0, DT_W - DT_COPIES * HEADS)).reshape(1, DT_W)


def kernel(x_prompt, x_sample, state_conv_a, state_conv_b, state_ssm, c_prompt, c_sample, w_mod, b_mod, norm_in_w, w_in, conv_a_w, norm_a_w, conv_b_w, conv_b_b, dt_bias, a_log, d_skip, norm_b_w, w_out, norm_f_w):
    depth = w_in.shape[0]
    assert depth == 1, "single-layer step"
    assert w_in.shape[2] == D_IN_PROJ
    nb, nd = x_prompt.shape[0], x_sample.shape[0]

    n_rows = nb + nd
    rows_pad = -(-n_rows // 16) * 16
    c_all = jnp.pad(jnp.concatenate([c_prompt, c_sample], axis=0), ((0, rows_pad - n_rows), (0, 0)))
    mod = _mod_call(c_all.astype(_BF16), w_mod[0].astype(_BF16), b_mod[0].reshape(1, -1))
    mod3 = mod[:n_rows].reshape(n_rows, 3, D_MODEL)

    w_in_b = w_in[0].astype(_BF16)
    w_dt = w_in_b[:, OFF_DT:]
    w_dt3 = jnp.concatenate([w_dt] * DT_COPIES + [jnp.zeros((D_MODEL, DT_W - DT_COPIES * HEADS), _BF16)], axis=1)
    params = (
        norm_in_w[0].reshape(1, D_MODEL), w_in_b, w_dt3, conv_a_w[0], norm_a_w[0].reshape(1, D_A),
        conv_b_w[0], conv_b_b[0].reshape(1, D_XBC), _pad_heads(dt_bias[0]), _pad_heads(a_log[0]),
        jnp.repeat(d_skip[0], HEADDIM).reshape(1, D_B), norm_b_w[0].reshape(1, D_B),
        w_out[0].astype(_BF16), norm_f_w.reshape(1, D_MODEL),
    )

    f32 = x_prompt.dtype
    za = jnp.zeros((nb, CONV_A_W - 1, D_A), f32)
    zb = jnp.zeros((nb, CONV_B_W - 1, D_XBC), f32)
    zs = jnp.zeros((nb, HEADS * HEADDIM, D_STATE), state_ssm.dtype)
    y_p, ca_p, cb_p, ss_p = _layer_call(x_prompt, mod3[:nb], za, zb, zs, params, S=1, Lb=256, Q=128)
    y_s, ca_s, cb_s, ss_s = _layer_call(
        x_sample, mod3[nb:], state_conv_a[0], state_conv_b[0],
        state_ssm[0].reshape(nd, HEADS * HEADDIM, D_STATE), params, S=4, Lb=x_sample.shape[1], Q=64)

    shp = lambda a: a.reshape(1, a.shape[0], HEADS, HEADDIM, D_STATE)
    return (y_p, y_s, ca_p[None], cb_p[None], shp(ss_p), ca_s[None], cb_s[None], shp(ss_s))
```

```python
import functools

import jax
import jax.numpy as jnp
import numpy as np
from jax import lax
from jax.experimental import pallas as pl
from jax.experimental.pallas import tpu as pltpu

D_MODEL = 1024
D_A = 1024
D_B = 1024
CONV_A_W = 3
CONV_B_W = 4
HEADS = 16
HEADDIM = 64
GROUPS = 4
HEADS_PER_GROUP = HEADS // GROUPS
D_STATE = 128
D_BC = GROUPS * D_STATE
D_XBC = D_B + 2 * D_BC
D_IN_PROJ = 4 * D_A + D_B + D_XBC + HEADS
GROUP_W = HEADS_PER_GROUP * HEADDIM
EPS = 1e-5
LOG2_E = 1.4426950408889634

LANES = 128
HIST = 8
DT_COPIES = 3
DT_W = LANES

OFF_A = 0
OFF_Z = 4 * D_A
OFF_XBC = OFF_Z + D_B
OFF_DT = OFF_XBC + D_XBC

VMEM_LIMIT_BYTES = 56 * 1024 * 1024

_F32 = jnp.float32
_BF16 = jnp.bfloat16


def _dot(a, b):
    return jnp.dot(a, b, preferred_element_type=_F32)


def _dot_nt(a, b):
    return lax.dot_general(a, b, (((1,), (1,)), ((), ())), preferred_element_type=_F32)


def _dot_tn(a, b):
    return lax.dot_general(a, b, (((0,), (0,)), ((), ())), preferred_element_type=_F32)


def _rms_scale(v):
    return lax.rsqrt(jnp.mean(v * v, axis=-1, keepdims=True) + EPS)


def _split3(v):
    hi = v.astype(_BF16)
    r1 = v - hi.astype(_F32)
    mid = r1.astype(_BF16)
    lo = (r1 - mid.astype(_F32)).astype(_BF16)
    lane = lax.broadcasted_iota(jnp.int32, v.shape, 1)
    return jnp.where(lane < HEADS, hi, jnp.where(lane < 2 * HEADS, mid, lo))


def _mod_kernel(c_ref, w_ref, b_ref, o_ref):
    o_ref[...] = _dot(c_ref[...], w_ref[...]) + b_ref[...]


def _layer_kernel(x_ref, mod_ref, sa_ref, sb_ref, ss_ref,
                  nin_ref, win_ref, wdt_ref, caw_ref, naw_ref, cbw_ref, cbb_ref, dtb_ref, alog_ref,
                  dsk_ref, nbw_ref, wout_ref, nf_ref, e64_ref, eq_ref, tri_ref,
                  y_ref, oa_ref, ob_ref, os_ref,
                  ubuf, xbuf, st_ref, hb_ref, outa_ref, *, S, Lb, Q):
    t = pl.program_id(1)
    R = S * Lb

    @pl.when(t == 0)
    def _load_carried_state():
        for s in range(S):
            for hist_ref, state_ref, width in ((ubuf, sa_ref, CONV_A_W), (xbuf, sb_ref, CONV_B_W)):
                hist_ref[s] = jnp.zeros(hist_ref.shape[1:], _F32)
                hist_ref[s, HIST - (width - 1):HIST, :] = state_ref[s]
            for g in range(GROUPS):
                st_ref[s, g] = ss_ref[s, GROUP_W * g:GROUP_W * (g + 1), :].T

    def per_seq(fn):
        parts = [fn(s, slice(s * Lb, (s + 1) * Lb)) for s in range(S)]
        return parts[0] if S == 1 else jnp.concatenate(parts, axis=0)

    x = x_ref[...].reshape(R, D_MODEL)

    xn = x * _rms_scale(x)
    hb_ref[...] = per_seq(
        lambda s, rows: xn[rows] * (nin_ref[...] * (1.0 + mod_ref[s, 1:2, :])) + mod_ref[s, 0:1, :]).astype(_BF16)
    hb = hb_ref[...]

    def causal_conv(val, hist_ref, s, w_ref):
        width = w_ref.shape[0]
        ext = jnp.concatenate([hist_ref[s], val], axis=0)
        acc = val * w_ref[width - 1:width, :]
        for d in range(1, width):
            acc = acc + pltpu.roll(ext, d, axis=0)[HIST:HIST + Lb] * w_ref[width - 1 - d:width - d, :]
        hist_ref[s] = val[Lb - HIST:Lb]
        return acc

    xr = _dot(hb, win_ref[:, OFF_XBC:OFF_XBC + D_XBC])
    dtr = _dot(hb, wdt_ref[...])
    xbc = jax.nn.silu(per_seq(lambda s, rows: causal_conv(xr[rows], xbuf, s, cbw_ref)) + cbb_ref[...])
    xs = xbc[:, 0:D_B]
    bm = xbc[:, D_B:D_B + D_BC].astype(_BF16)
    cm = xbc[:, D_B + D_BC:D_XBC].astype(_BF16)

    pa = _dot(hb, win_ref[:, OFF_A:OFF_A + 4 * D_A])
    b_gate = pa[:, 0:D_A]
    u = pa[:, D_A:2 * D_A] * pa[:, 2 * D_A:3 * D_A]
    g_a = pa[:, 3 * D_A:4 * D_A]
    va = b_gate * per_seq(lambda s, rows: causal_conv(u[rows], ubuf, s, caw_ref)) * jax.nn.silu(g_a)
    ya = (va * _rms_scale(va) * naw_ref[...]).astype(_BF16)
    outa_ref[...] = _dot(ya, wout_ref[0:D_A, :])

    first = va[0:HIST, :]
    folded = sum(first[:, c:c + LANES] for c in range(0, D_A, LANES))
    half = jnp.uint32(16)
    zero = pltpu.bitcast(lax.shift_right_logical(
        lax.shift_right_logical(pltpu.bitcast(folded, jnp.uint32), half), half), _F32)
    hb_ref[0:2 * HIST, 0:LANES] = (hb_ref[0:2 * HIST, 0:LANES].astype(_F32)
                                   + jnp.concatenate([zero, zero], axis=0)).astype(_BF16)
    z = _dot(hb_ref[...], win_ref[:, OFF_Z:OFF_Z + D_B])

    dt = jax.nn.softplus(dtr + dtb_ref[...])
    da = dt * (-jnp.exp(alog_ref[...]))
    tri = tri_ref[...]
    da_hi = da.astype(_BF16)
    da_r1 = da - da_hi.astype(_F32)
    da_mid = da_r1.astype(_BF16)
    da_lo = (da_r1 - da_mid.astype(_F32)).astype(_BF16)
    acum = _dot(tri, da_hi) + _dot(tri, da_mid) + _dot(tri, da_lo)
    nchunks = R // Q
    alast = jnp.broadcast_to(acum.reshape(nchunks, Q, LANES)[:, Q - 1:Q, :],
                             (nchunks, Q, LANES)).reshape(R, LANES)
    ea_x = _dot(_split3(jnp.exp(acum)), e64_ref[...])
    w_x = _dot(_split3(dt * jnp.exp(alast - acum)), e64_ref[...])
    acum2 = acum * LOG2_E
    a_x = _dot(_split3(acum2), eq_ref[...])
    acum_t = acum2.T
    dt_t = dt.T

    row_i = lax.broadcasted_iota(jnp.int32, (Q, HEADS_PER_GROUP * Q), 0)
    col_i = lax.broadcasted_iota(jnp.int32, (Q, HEADS_PER_GROUP * Q), 1)
    causal = row_i >= (col_i % Q)
    bd_row = lax.broadcasted_iota(jnp.int32, (HEADS_PER_GROUP * Q, GROUP_W), 0) // Q
    bd_col = lax.broadcasted_iota(jnp.int32, (HEADS_PER_GROUP * Q, GROUP_W), 1) // HEADDIM
    blockdiag = bd_row == bd_col

    y_rows = []
    for s in range(S):
        for c in range(Lb // Q):
            r0 = s * Lb + c * Q
            rows = slice(r0, r0 + Q)
            y_groups = []
            for g in range(GROUPS):
                gx = slice(GROUP_W * g, GROUP_W * (g + 1))
                gn = slice(D_STATE * g, D_STATE * (g + 1))
                heads = range(HEADS_PER_GROUP * g, HEADS_PER_GROUP * (g + 1))
                cg = cm[rows, gn]
                bg = bm[rows, gn]
                cb = _dot_nt(cg, bg)
                cb4 = jnp.concatenate([cb] * HEADS_PER_GROUP, axis=1)
                arow = jnp.concatenate([acum_t[h:h + 1, rows] for h in heads], axis=1)
                dtrow = jnp.concatenate([dt_t[h:h + 1, rows] for h in heads], axis=1)
                seg = a_x[rows, HEADS_PER_GROUP * Q * g:HEADS_PER_GROUP * Q * (g + 1)] - arow
                mp = cb4 * jnp.exp2(jnp.where(causal, seg, -jnp.inf)) * dtrow
                xg = xs[rows, gx]
                xg_b = xg.astype(_BF16)
                bd = jnp.where(blockdiag, jnp.concatenate([xg_b] * HEADS_PER_GROUP, axis=0),
                               jnp.zeros((), _BF16))
                y_diag = _dot(mp.astype(_BF16), bd)
                st = st_ref[s, g]
                y_off = _dot(cg, st.astype(_BF16)) * ea_x[rows, gx]
                y_groups.append(y_diag + y_off)
                xw = (xg * w_x[rows, gx]).astype(_BF16)
                st_ref[s, g] = st * ea_x[r0 + Q - 1:r0 + Q, gx] + _dot_tn(bg, xw)
            y_rows.append(jnp.concatenate(y_groups, axis=1))
    y = y_rows[0] if len(y_rows) == 1 else jnp.concatenate(y_rows, axis=0)

    y = y + dsk_ref[...] * xs
    vb = y * jax.nn.silu(z)
    yb = (vb * _rms_scale(vb) * nbw_ref[...]).astype(_BF16)

    out = outa_ref[...] + _dot(yb, wout_ref[D_A:D_A + D_B, :])
    res = per_seq(lambda s, rows: x[rows] + mod_ref[s, 2:3, :] * out[rows])
    y_ref[...] = (res * _rms_scale(res) * nf_ref[...]).reshape(S, Lb, D_MODEL)

    @pl.when(t == pl.num_programs(1) - 1)
    def _store_carried_state():
        for s in range(S):
            oa_ref[s] = ubuf[s, HIST - (CONV_A_W - 1):HIST, :]
            ob_ref[s] = xbuf[s, HIST - (CONV_B_W - 1):HIST, :]
            for g in range(GROUPS):
                os_ref[s, GROUP_W * g:GROUP_W * (g + 1), :] = st_ref[s, g].T


def _expand_matrix(lanes_per_head):
    e = np.zeros((LANES, HEADS * lanes_per_head), np.float32)
    for k in range(DT_COPIES):
        for h in range(HEADS):
            e[k * HEADS + h, h * lanes_per_head:(h + 1) * lanes_per_head] = 1.0
    return jnp.asarray(e, _BF16)


def _chunk_tri(rows, q):
    i = np.arange(rows)
    return jnp.asarray((i[:, None] // q == i[None, :] // q) & (i[None, :] <= i[:, None]), _BF16)


def _const_spec(shape):
    return pl.BlockSpec(shape, lambda i, j: (0,) * len(shape), pipeline_mode=pl.Buffered(1))


def _layer_call(x, mod3, sa, sb, ss, params, *, S, Lb, Q):
    ns, L, _ = x.shape
    R = S * Lb
    consts = (_expand_matrix(HEADDIM), _expand_matrix(Q), _chunk_tri(R, Q))
    seq3 = lambda i, j: (i, 0, 0)
    in_specs = [
        pl.BlockSpec((S, Lb, D_MODEL), lambda i, j: (i, j, 0)),
        pl.BlockSpec((S, 3, D_MODEL), seq3),
        pl.BlockSpec((S, CONV_A_W - 1, D_A), seq3),
        pl.BlockSpec((S, CONV_B_W - 1, D_XBC), seq3),
        pl.BlockSpec((S, HEADS * HEADDIM, D_STATE), seq3),
    ] + [_const_spec(p.shape) for p in params + consts]
    out_shape = (
        jax.ShapeDtypeStruct((ns, L, D_MODEL), _F32),
        jax.ShapeDtypeStruct((ns, CONV_A_W - 1, D_A), _F32),
        jax.ShapeDtypeStruct((ns, CONV_B_W - 1, D_XBC), _F32),
        jax.ShapeDtypeStruct((ns, HEADS * HEADDIM, D_STATE), _F32),
    )
    out_specs = (
        pl.BlockSpec((S, Lb, D_MODEL), lambda i, j: (i, j, 0)),
        pl.BlockSpec((S, CONV_A_W - 1, D_A), seq3),
        pl.BlockSpec((S, CONV_B_W - 1, D_XBC), seq3),
        pl.BlockSpec((S, HEADS * HEADDIM, D_STATE), seq3),
    )
    scratch = [
        pltpu.VMEM((S, HIST, D_A), _F32),
        pltpu.VMEM((S, HIST, D_XBC), _F32),
        pltpu.VMEM((S, GROUPS, D_STATE, GROUP_W), _F32),
        pltpu.VMEM((R, D_MODEL), _BF16),
        pltpu.VMEM((R, D_MODEL), _F32),
    ]
    return pl.pallas_call(
        functools.partial(_layer_kernel, S=S, Lb=Lb, Q=Q),
        grid=(ns // S, L // Lb),
        in_specs=in_specs,
        out_specs=out_specs,
        out_shape=out_shape,
        scratch_shapes=scratch,
        compiler_params=pltpu.CompilerParams(
            dimension_semantics=("arbitrary", "arbitrary"),
            vmem_limit_bytes=VMEM_LIMIT_BYTES),
        name=f"layer_s{S}_l{Lb}",
    )(x, mod3, sa, sb, ss, *params, *consts)


def _mod_call(c_all, w_mod, b_mod):
    rows = c_all.shape[0]
    n = w_mod.shape[1]
    bn = D_MODEL
    return pl.pallas_call(
        _mod_kernel,
        grid=(n // bn,),
        in_specs=[pl.BlockSpec((rows, D_MODEL), lambda j: (0, 0)),
                  pl.BlockSpec((D_MODEL, bn), lambda j: (0, j)),
                  pl.BlockSpec((1, bn), lambda j: (0, j))],
        out_specs=pl.BlockSpec((rows, bn), lambda j: (0, j)),
        out_shape=jax.ShapeDtypeStruct((rows, n), _F32),
        name="mod_proj",
    )(c_all, w_mod, b_mod)


def _pad_heads(v):
    return jnp.pad(jnp.tile(v, DT_COPIES), (0, DT_W - DT_COPIES * HEADS)).reshape(1, DT_W)


def kernel(x_prompt, x_sample, state_conv_a, state_conv_b, state_ssm, c_prompt, c_sample, w_mod, b_mod, norm_in_w, w_in, conv_a_w, norm_a_w, conv_b_w, conv_b_b, dt_bias, a_log, d_skip, norm_b_w, w_out, norm_f_w):
    depth = w_in.shape[0]
    assert depth == 1, "single-layer step"
    assert w_in.shape[2] == D_IN_PROJ
    nb, nd = x_prompt.shape[0], x_sample.shape[0]

    n_rows = nb + nd
    rows_pad = -(-n_rows // 16) * 16
    c_all = jnp.pad(jnp.concatenate([c_prompt, c_sample], axis=0), ((0, rows_pad - n_rows), (0, 0)))
    mod = _mod_call(c_all.astype(_BF16), w_mod[0].astype(_BF16), b_mod[0].reshape(1, -1))
    mod3 = mod[:n_rows].reshape(n_rows, 3, D_MODEL)

    w_in_b = w_in[0].astype(_BF16)
    w_dt = w_in_b[:, OFF_DT:]
    w_dt3 = jnp.concatenate([w_dt] * DT_COPIES + [jnp.zeros((D_MODEL, DT_W - DT_COPIES * HEADS), _BF16)], axis=1)
    params = (
        norm_in_w[0].reshape(1, D_MODEL), w_in_b, w_dt3, conv_a_w[0], norm_a_w[0].reshape(1, D_A),
        conv_b_w[0], conv_b_b[0].reshape(1, D_XBC), _pad_heads(dt_bias[0]), _pad_heads(a_log[0]),
        jnp.repeat(d_skip[0], HEADDIM).reshape(1, D_B), norm_b_w[0].reshape(1, D_B),
        w_out[0].astype(_BF16), norm_f_w.reshape(1, D_MODEL),
    )

    f32 = x_prompt.dtype
    za = jnp.zeros((nb, CONV_A_W - 1, D_A), f32)
    zb = jnp.zeros((nb, CONV_B_W - 1, D_XBC), f32)
    zs = jnp.zeros((nb, HEADS * HEADDIM, D_STATE), state_ssm.dtype)
    y_p, ca_p, cb_p, ss_p = _layer_call(x_prompt, mod3[:nb], za, zb, zs, params, S=1, Lb=256, Q=128)
    y_s, ca_s, cb_s, ss_s = _layer_call(
        x_sample, mod3[nb:], state_conv_a[0], state_conv_b[0],
        state_ssm[0].reshape(nd, HEADS * HEADDIM, D_STATE), params, S=4, Lb=x_sample.shape[1], Q=64)

    shp = lambda a: a.reshape(1, a.shape[0], HEADS, HEADDIM, D_STATE)
    return (y_p, y_s, ca_p[None], cb_p[None], shp(ss_p), ca_s[None], cb_s[None], shp(ss_s))
```

```python
import functools

import jax
import jax.numpy as jnp
import numpy as np
from jax import lax
from jax.experimental import pallas as pl
from jax.experimental.pallas import tpu as pltpu

D_MODEL = 1024
D_A = 1024
D_B = 1024
CONV_A_W = 3
CONV_B_W = 4
HEADS = 16
HEADDIM = 64
GROUPS = 4
HEADS_PER_GROUP = HEADS // GROUPS
D_STATE = 128
D_BC = GROUPS * D_STATE
D_XBC = D_B + 2 * D_BC
D_IN_PROJ = 4 * D_A + D_B + D_XBC + HEADS
GROUP_W = HEADS_PER_GROUP * HEADDIM
EPS = 1e-5
LOG2_E = 1.4426950408889634

LANES = 128
MXU_W = 256
HIST = 8
DT_COPIES = 3
DT_W = LANES

OFF_A = 0
OFF_Z = 4 * D_A
OFF_XBC = OFF_Z + D_B
OFF_DT = OFF_XBC + D_XBC

VMEM_LIMIT_BYTES = 56 * 1024 * 1024

_F32 = jnp.float32
_BF16 = jnp.bfloat16


def _dot(a, b):
    return jnp.dot(a, b, preferred_element_type=_F32)


def _dot_nt(a, b):
    return lax.dot_general(a, b, (((1,), (1,)), ((), ())), preferred_element_type=_F32)


def _dot_tn(a, b):
    return lax.dot_general(a, b, (((0,), (0,)), ((), ())), preferred_element_type=_F32)


def _rms_scale(v):
    return lax.rsqrt(jnp.mean(v * v, axis=-1, keepdims=True) + EPS)


def _split3(v):
    hi = v.astype(_BF16)
    r1 = v - hi.astype(_F32)
    mid = r1.astype(_BF16)
    lo = (r1 - mid.astype(_F32)).astype(_BF16)
    lane = lax.broadcasted_iota(jnp.int32, v.shape, 1)
    return jnp.where(lane < HEADS, hi, jnp.where(lane < 2 * HEADS, mid, lo))


def _mod_kernel(c_ref, w_ref, b_ref, o_ref):
    o_ref[...] = _dot(c_ref[...], w_ref[...]) + b_ref[...]


def _layer_kernel(x_ref, mod_ref, sa_ref, sb_ref, ss_ref,
                  nin_ref, win_ref, wdt_ref, caw_ref, naw_ref, cbw_ref, cbb_ref, dtb_ref, alog_ref,
                  dsk_ref, nbw_ref, wout_ref, nf_ref, e64_ref, eq_ref, tri_ref,
                  y_ref, oa_ref, ob_ref, os_ref,
                  ubuf, xbuf, st_ref, z_ref, outa_ref, *, S, Lb, Q):
    t = pl.program_id(1)
    R = S * Lb

    @pl.when(t == 0)
    def _load_carried_state():
        for s in range(S):
            for hist_ref, state_ref, width in ((ubuf, sa_ref, CONV_A_W), (xbuf, sb_ref, CONV_B_W)):
                hist_ref[s] = jnp.zeros(hist_ref.shape[1:], _F32)
                hist_ref[s, HIST - (width - 1):HIST, :] = state_ref[s]
            for g in range(GROUPS):
                st_ref[s, g] = ss_ref[s, GROUP_W * g:GROUP_W * (g + 1), :].T

    def per_seq(fn):
        parts = [fn(s, slice(s * Lb, (s + 1) * Lb)) for s in range(S)]
        return parts[0] if S == 1 else jnp.concatenate(parts, axis=0)

    x = x_ref[...].reshape(R, D_MODEL)

    xn = x * _rms_scale(x)
    hb = per_seq(lambda s, rows: xn[rows] * (nin_ref[...] * (1.0 + mod_ref[s, 1:2, :])) + mod_ref[s, 0:1, :]).astype(_BF16)

    def causal_conv(val, hist_ref, s, w_ref):
        width = w_ref.shape[0]
        ext = jnp.concatenate([hist_ref[s], val], axis=0)
        acc = val * w_ref[width - 1:width, :]
        for d in range(1, width):
            acc = acc + pltpu.roll(ext, d, axis=0)[HIST:HIST + Lb] * w_ref[width - 1 - d:width - d, :]
        hist_ref[s] = val[Lb - HIST:Lb]
        return acc

    xr = _dot(hb, win_ref[:, OFF_XBC:OFF_XBC + D_XBC])
    dtr = _dot(hb, wdt_ref[...])
    xbc = jax.nn.silu(per_seq(lambda s, rows: causal_conv(xr[rows], xbuf, s, cbw_ref)) + cbb_ref[...])
    xs = xbc[:, 0:D_B]
    bm = xbc[:, D_B:D_B + D_BC].astype(_BF16)
    cm = xbc[:, D_B + D_BC:D_XBC].astype(_BF16)

    pa = _dot(hb, win_ref[:, OFF_A:OFF_A + 4 * D_A])
    b_gate = pa[:, 0:D_A]
    u = pa[:, D_A:2 * D_A] * pa[:, 2 * D_A:3 * D_A]
    g_a = pa[:, 3 * D_A:4 * D_A]
    va = b_gate * per_seq(lambda s, rows: causal_conv(u[rows], ubuf, s, caw_ref)) * jax.nn.silu(g_a)
    ya = (va * _rms_scale(va) * naw_ref[...]).astype(_BF16)

    def z_chunk(c):
        z_ref[:, c:c + MXU_W] = _dot(hb, win_ref[:, OFF_Z + c:OFF_Z + c + MXU_W])

    def outa_chunk(c):
        outa_ref[:, c:c + MXU_W] = _dot(ya, wout_ref[0:D_A, c:c + MXU_W])

    fillers = ([functools.partial(z_chunk, c) for c in range(0, D_B, MXU_W)]
               + [functools.partial(outa_chunk, c) for c in range(0, D_MODEL, MXU_W)])

    dt = jax.nn.softplus(dtr + dtb_ref[...])
    da = dt * (-jnp.exp(alog_ref[...]))
    tri = tri_ref[...]
    da_hi = da.astype(_BF16)
    da_r1 = da - da_hi.astype(_F32)
    da_mid = da_r1.astype(_BF16)
    da_lo = (da_r1 - da_mid.astype(_F32)).astype(_BF16)
    acum = _dot(tri, da_hi) + _dot(tri, da_mid) + _dot(tri, da_lo)
    nchunks = R // Q
    alast = jnp.broadcast_to(acum.reshape(nchunks, Q, LANES)[:, Q - 1:Q, :],
                             (nchunks, Q, LANES)).reshape(R, LANES)
    ea_x = _dot(_split3(jnp.exp(acum)), e64_ref[...])
    w_x = _dot(_split3(dt * jnp.exp(alast - acum)), e64_ref[...])
    acum2 = acum * LOG2_E
    a_x = _dot(_split3(acum2), eq_ref[...])
    acum_t = acum2.T
    dt_t = dt.T

    row_i = lax.broadcasted_iota(jnp.int32, (Q, HEADS_PER_GROUP * Q), 0)
    col_i = lax.broadcasted_iota(jnp.int32, (Q, HEADS_PER_GROUP * Q), 1)
    causal = row_i >= (col_i % Q)
    bd_row = lax.broadcasted_iota(jnp.int32, (HEADS_PER_GROUP * Q, GROUP_W), 0) // Q
    bd_col = lax.broadcasted_iota(jnp.int32, (HEADS_PER_GROUP * Q, GROUP_W), 1) // HEADDIM
    blockdiag = bd_row == bd_col

    y_rows = []
    for s in range(S):
        for c in range(Lb // Q):
            r0 = s * Lb + c * Q
            rows = slice(r0, r0 + Q)
            y_groups = []
            for g in range(GROUPS):
                gx = slice(GROUP_W * g, GROUP_W * (g + 1))
                gn = slice(D_STATE * g, D_STATE * (g + 1))
                heads = range(HEADS_PER_GROUP * g, HEADS_PER_GROUP * (g + 1))
                cg = cm[rows, gn]
                bg = bm[rows, gn]
                cb = _dot_nt(cg, bg)
                cb4 = jnp.concatenate([cb] * HEADS_PER_GROUP, axis=1)
                arow = jnp.concatenate([acum_t[h:h + 1, rows] for h in heads], axis=1)
                dtrow = jnp.concatenate([dt_t[h:h + 1, rows] for h in heads], axis=1)
                seg = a_x[rows, HEADS_PER_GROUP * Q * g:HEADS_PER_GROUP * Q * (g + 1)] - arow
                mp = cb4 * jnp.exp2(jnp.where(causal, seg, -jnp.inf)) * dtrow
                xg = xs[rows, gx]
                xg_b = xg.astype(_BF16)
                bd = jnp.where(blockdiag, jnp.concatenate([xg_b] * HEADS_PER_GROUP, axis=0),
                               jnp.zeros((), _BF16))
                y_diag = _dot(mp.astype(_BF16), bd)
                st = st_ref[s, g]
                y_off = _dot(cg, st.astype(_BF16)) * ea_x[rows, gx]
                y_groups.append(y_diag + y_off)
                xw = (xg * w_x[rows, gx]).astype(_BF16)
                st_ref[s, g] = st * ea_x[r0 + Q - 1:r0 + Q, gx] + _dot_tn(bg, xw)
                if fillers:
                    fillers.pop(0)()
            y_rows.append(jnp.concatenate(y_groups, axis=1))
    y = y_rows[0] if len(y_rows) == 1 else jnp.concatenate(y_rows, axis=0)

    while fillers:
        fillers.pop(0)()
    y = y + dsk_ref[...] * xs
    vb = y * jax.nn.silu(z_ref[...])
    yb = (vb * _rms_scale(vb) * nbw_ref[...]).astype(_BF16)

    out = outa_ref[...] + _dot(yb, wout_ref[D_A:D_A + D_B, :])
    res = per_seq(lambda s, rows: x[rows] + mod_ref[s, 2:3, :] * out[rows])
    y_ref[...] = (res * _rms_scale(res) * nf_ref[...]).reshape(S, Lb, D_MODEL)

    @pl.when(t == pl.num_programs(1) - 1)
    def _store_carried_state():
        for s in range(S):
            oa_ref[s] = ubuf[s, HIST - (CONV_A_W - 1):HIST, :]
            ob_ref[s] = xbuf[s, HIST - (CONV_B_W - 1):HIST, :]
            for g in range(GROUPS):
                os_ref[s, GROUP_W * g:GROUP_W * (g + 1), :] = st_ref[s, g].T


def _expand_matrix(lanes_per_head):
    e = np.zeros((LANES, HEADS * lanes_per_head), np.float32)
    for k in range(DT_COPIES):
        for h in range(HEADS):
            e[k * HEADS + h, h * lanes_per_head:(h + 1) * lanes_per_head] = 1.0
    return jnp.asarray(e, _BF16)


def _chunk_tri(rows, q):
    i = np.arange(rows)
    return jnp.asarray((i[:, None] // q == i[None, :] // q) & (i[None, :] <= i[:, None]), _BF16)


def _const_spec(shape):
    return pl.BlockSpec(shape, lambda i, j: (0,) * len(shape), pipeline_mode=pl.Buffered(1))


def _layer_call(x, mod3, sa, sb, ss, params, *, S, Lb, Q):
    ns, L, _ = x.shape
    R = S * Lb
    consts = (_expand_matrix(HEADDIM), _expand_matrix(Q), _chunk_tri(R, Q))
    seq3 = lambda i, j: (i, 0, 0)
    in_specs = [
        pl.BlockSpec((S, Lb, D_MODEL), lambda i, j: (i, j, 0)),
        pl.BlockSpec((S, 3, D_MODEL), seq3),
        pl.BlockSpec((S, CONV_A_W - 1, D_A), seq3),
        pl.BlockSpec((S, CONV_B_W - 1, D_XBC), seq3),
        pl.BlockSpec((S, HEADS * HEADDIM, D_STATE), seq3),
    ] + [_const_spec(p.shape) for p in params + consts]
    out_shape = (
        jax.ShapeDtypeStruct((ns, L, D_MODEL), _F32),
        jax.ShapeDtypeStruct((ns, CONV_A_W - 1, D_A), _F32),
        jax.ShapeDtypeStruct((ns, CONV_B_W - 1, D_XBC), _F32),
        jax.ShapeDtypeStruct((ns, HEADS * HEADDIM, D_STATE), _F32),
    )
    out_specs = (
        pl.BlockSpec((S, Lb, D_MODEL), lambda i, j: (i, j, 0)),
        pl.BlockSpec((S, CONV_A_W - 1, D_A), seq3),
        pl.BlockSpec((S, CONV_B_W - 1, D_XBC), seq3),
        pl.BlockSpec((S, HEADS * HEADDIM, D_STATE), seq3),
    )
    scratch = [
        pltpu.VMEM((S, HIST, D_A), _F32),
        pltpu.VMEM((S, HIST, D_XBC), _F32),
        pltpu.VMEM((S, GROUPS, D_STATE, GROUP_W), _F32),
        pltpu.VMEM((R, D_B), _F32),
        pltpu.VMEM((R, D_MODEL), _F32),
    ]
    return pl.pallas_call(
        functools.partial(_layer_kernel, S=S, Lb=Lb, Q=Q),
        grid=(ns // S, L // Lb),
        in_specs=in_specs,
        out_specs=out_specs,
        out_shape=out_shape,
        scratch_shapes=scratch,
        compiler_params=pltpu.CompilerParams(
            dimension_semantics=("arbitrary", "arbitrary"),
            vmem_limit_bytes=VMEM_LIMIT_BYTES),
        name=f"layer_s{S}_l{Lb}",
    )(x, mod3, sa, sb, ss, *params, *consts)


def _mod_call(c_all, w_mod, b_mod):
    rows = c_all.shape[0]
    n = w_mod.shape[1]
    bn = D_MODEL
    return pl.pallas_call(
        _mod_kernel,
        grid=(n // bn,),
        in_specs=[pl.BlockSpec((rows, D_MODEL), lambda j: (0, 0)),
                  pl.BlockSpec((D_MODEL, bn), lambda j: (0, j)),
                  pl.BlockSpec((1, bn), lambda j: (0, j))],
        out_specs=pl.BlockSpec((rows, bn), lambda j: (0, j)),
        out_shape=jax.ShapeDtypeStruct((rows, n), _F32),
        name="mod_proj",
    )(c_all, w_mod, b_mod)


def _pad_heads(v):
    return jnp.pad(jnp.tile(v, DT_COPIES), (0, DT_W - DT_COPIES * HEADS)).reshape(1, DT_W)


def kernel(x_prompt, x_sample, state_conv_a, state_conv_b, state_ssm, c_prompt, c_sample, w_mod, b_mod, norm_in_w, w_in, conv_a_w, norm_a_w, conv_b_w, conv_b_b, dt_bias, a_log, d_skip, norm_b_w, w_out, norm_f_w):
    depth = w_in.shape[0]
    assert depth == 1, "single-layer step"
    assert w_in.shape[2] == D_IN_PROJ
    nb, nd = x_prompt.shape[0], x_sample.shape[0]

    n_rows = nb + nd
    rows_pad = -(-n_rows // 16) * 16
    c_all = jnp.pad(jnp.concatenate([c_prompt, c_sample], axis=0), ((0, rows_pad - n_rows), (0, 0)))
    mod = _mod_call(c_all.astype(_BF16), w_mod[0].astype(_BF16), b_mod[0].reshape(1, -1))
    mod3 = mod[:n_rows].reshape(n_rows, 3, D_MODEL)

    w_in_b = w_in[0].astype(_BF16)
    w_dt = w_in_b[:, OFF_DT:]
    w_dt3 = jnp.concatenate([w_dt] * DT_COPIES + [jnp.zeros((D_MODEL, DT_W - DT_COPIES * HEADS), _BF16)], axis=1)
    params = (
        norm_in_w[0].reshape(1, D_MODEL), w_in_b, w_dt3, conv_a_w[0], norm_a_w[0].reshape(1, D_A),
        conv_b_w[0], conv_b_b[0].reshape(1, D_XBC), _pad_heads(dt_bias[0]), _pad_heads(a_log[0]),
        jnp.repeat(d_skip[0], HEADDIM).reshape(1, D_B), norm_b_w[0].reshape(1, D_B),
        w_out[0].astype(_BF16), norm_f_w.reshape(1, D_MODEL),
    )

    f32 = x_prompt.dtype
    za = jnp.zeros((nb, CONV_A_W - 1, D_A), f32)
    zb = jnp.zeros((nb, CONV_B_W - 1, D_XBC), f32)
    zs = jnp.zeros((nb, HEADS * HEADDIM, D_STATE), state_ssm.dtype)
    y_p, ca_p, cb_p, ss_p = _layer_call(x_prompt, mod3[:nb], za, zb, zs, params, S=1, Lb=256, Q=128)
    y_s, ca_s, cb_s, ss_s = _layer_call(
        x_sample, mod3[nb:], state_conv_a[0], state_conv_b[0],
        state_ssm[0].reshape(nd, HEADS * HEADDIM, D_STATE), params, S=4, Lb=x_sample.shape[1], Q=64)

    shp = lambda a: a.reshape(1, a.shape[0], HEADS, HEADDIM, D_STATE)
    return (y_p, y_s, ca_p[None], cb_p[None], shp(ss_p), ca_s[None], cb_s[None], shp(ss_s))
```

```python
import functools

import jax
import jax.numpy as jnp
import numpy as np
from jax import lax
from jax.experimental import pallas as pl
from jax.experimental.pallas import tpu as pltpu

D_MODEL = 1024
D_A = 1024
D_B = 1024
CONV_A_W = 3
CONV_B_W = 4
HEADS = 16
HEADDIM = 64
GROUPS = 4
HEADS_PER_GROUP = HEADS // GROUPS
D_STATE = 128
D_BC = GROUPS * D_STATE
D_XBC = D_B + 2 * D_BC
D_IN_PROJ = 4 * D_A + D_B + D_XBC + HEADS
GROUP_W = HEADS_PER_GROUP * HEADDIM
EPS = 1e-5
LOG2_E = 1.4426950408889634

LANES = 128
MXU_W = 256
HIST = 8
DT_COPIES = 3
DT_W = LANES

OFF_A = 0
OFF_Z = 4 * D_A
OFF_XBC = OFF_Z + D_B
OFF_DT = OFF_XBC + D_XBC

VMEM_LIMIT_BYTES = 56 * 1024 * 1024

_F32 = jnp.float32
_BF16 = jnp.bfloat16


def _dot(a, b):
    return jnp.dot(a, b, preferred_element_type=_F32)


def _dot_nt(a, b):
    return lax.dot_general(a, b, (((1,), (1,)), ((), ())), preferred_element_type=_F32)


def _dot_tn(a, b):
    return lax.dot_general(a, b, (((0,), (0,)), ((), ())), preferred_element_type=_F32)


def _rms_scale(v):
    return lax.rsqrt(jnp.mean(v * v, axis=-1, keepdims=True) + EPS)


def _split3(v):
    hi = v.astype(_BF16)
    r1 = v - hi.astype(_F32)
    mid = r1.astype(_BF16)
    lo = (r1 - mid.astype(_F32)).astype(_BF16)
    lane = lax.broadcasted_iota(jnp.int32, v.shape, 1)
    return jnp.where(lane < HEADS, hi, jnp.where(lane < 2 * HEADS, mid, lo))


def _mod_kernel(c_ref, w_ref, b_ref, o_ref):
    o_ref[...] = _dot(c_ref[...], w_ref[...]) + b_ref[...]


def _layer_kernel(x_ref, mod_ref, sa_ref, sb_ref, ss_ref,
                  nin_ref, win_ref, wdt_ref, caw_ref, naw_ref, cbw_ref, cbb_ref, dtb_ref, alog_ref,
                  dsk_ref, nbw_ref, wout_ref, nf_ref, e64_ref, eq_ref, tri_ref,
                  y_ref, oa_ref, ob_ref, os_ref,
                  ubuf, xbuf, st_ref, z_ref, outa_ref, *, S, Lb, Q):
    t = pl.program_id(1)
    R = S * Lb

    @pl.when(t == 0)
    def _load_carried_state():
        for s in range(S):
            for hist_ref, state_ref, width in ((ubuf, sa_ref, CONV_A_W), (xbuf, sb_ref, CONV_B_W)):
                hist_ref[s] = jnp.zeros(hist_ref.shape[1:], _F32)
                hist_ref[s, HIST - (width - 1):HIST, :] = state_ref[s]
            for g in range(GROUPS):
                st_ref[s, g] = ss_ref[s, GROUP_W * g:GROUP_W * (g + 1), :].T

    def per_seq(fn):
        parts = [fn(s, slice(s * Lb, (s + 1) * Lb)) for s in range(S)]
        return parts[0] if S == 1 else jnp.concatenate(parts, axis=0)

    x = x_ref[...].reshape(R, D_MODEL)

    xn = x * _rms_scale(x)
    hb = per_seq(lambda s, rows: xn[rows] * (nin_ref[...] * (1.0 + mod_ref[s, 1:2, :])) + mod_ref[s, 0:1, :]).astype(_BF16)

    def causal_conv(val, hist_ref, s, w_ref):
        width = w_ref.shape[0]
        ext = jnp.concatenate([hist_ref[s], val], axis=0)
        acc = val * w_ref[width - 1:width, :]
        for d in range(1, width):
            acc = acc + pltpu.roll(ext, d, axis=0)[HIST:HIST + Lb] * w_ref[width - 1 - d:width - d, :]
        hist_ref[s] = val[Lb - HIST:Lb]
        return acc

    xr = _dot(hb, win_ref[:, OFF_XBC:OFF_XBC + D_XBC])
    dtr = _dot(hb, wdt_ref[...])
    xbc = jax.nn.silu(per_seq(lambda s, rows: causal_conv(xr[rows], xbuf, s, cbw_ref)) + cbb_ref[...])
    xs = xbc[:, 0:D_B]
    bm = xbc[:, D_B:D_B + D_BC].astype(_BF16)
    cm = xbc[:, D_B + D_BC:D_XBC].astype(_BF16)

    pa = _dot(hb, win_ref[:, OFF_A:OFF_A + 4 * D_A])
    b_gate = pa[:, 0:D_A]
    u = pa[:, D_A:2 * D_A] * pa[:, 2 * D_A:3 * D_A]
    g_a = pa[:, 3 * D_A:4 * D_A]
    va = b_gate * per_seq(lambda s, rows: causal_conv(u[rows], ubuf, s, caw_ref)) * jax.nn.silu(g_a)
    ya = (va * _rms_scale(va) * naw_ref[...]).astype(_BF16)

    def z_chunk(c):
        z_ref[:, c:c + MXU_W] = _dot(hb, win_ref[:, OFF_Z + c:OFF_Z + c + MXU_W])

    def outa_chunk(c):
        outa_ref[:, c:c + MXU_W] = _dot(ya, wout_ref[0:D_A, c:c + MXU_W])

    fillers = ([functools.partial(z_chunk, c) for c in range(0, D_B, MXU_W)]
               + [functools.partial(outa_chunk, c) for c in range(0, D_MODEL, MXU_W)])

    dt = jax.nn.softplus(dtr + dtb_ref[...])
    da = dt * (-jnp.exp(alog_ref[...]))
    tri = tri_ref[...]
    da_hi = da.astype(_BF16)
    da_r1 = da - da_hi.astype(_F32)
    da_mid = da_r1.astype(_BF16)
    da_lo = (da_r1 - da_mid.astype(_F32)).astype(_BF16)
    acum = _dot(tri, da_hi) + _dot(tri, da_mid) + _dot(tri, da_lo)
    nchunks = R // Q
    alast = jnp.broadcast_to(acum.reshape(nchunks, Q, LANES)[:, Q - 1:Q, :],
                             (nchunks, Q, LANES)).reshape(R, LANES)
    ea_x = _dot(_split3(jnp.exp(acum)), e64_ref[...])
    w_x = _dot(_split3(dt * jnp.exp(alast - acum)), e64_ref[...])
    acum2 = acum * LOG2_E
    a_x = _dot(_split3(acum2), eq_ref[...])
    acum_t = acum2.T
    dt_t = dt.T

    row_i = lax.broadcasted_iota(jnp.int32, (Q, HEADS_PER_GROUP * Q), 0)
    col_i = lax.broadcasted_iota(jnp.int32, (Q, HEADS_PER_GROUP * Q), 1)
    causal = row_i >= (col_i % Q)
    bd_row = lax.broadcasted_iota(jnp.int32, (HEADS_PER_GROUP * Q, GROUP_W), 0) // Q
    bd_col = lax.broadcasted_iota(jnp.int32, (HEADS_PER_GROUP * Q, GROUP_W), 1) // HEADDIM
    blockdiag = bd_row == bd_col

    y_rows = []
    for s in range(S):
        for c in range(Lb // Q):
            r0 = s * Lb + c * Q
            rows = slice(r0, r0 + Q)
            y_groups = []
            for g in range(GROUPS):
                gx = slice(GROUP_W * g, GROUP_W * (g + 1))
                gn = slice(D_STATE * g, D_STATE * (g + 1))
                heads = range(HEADS_PER_GROUP * g, HEADS_PER_GROUP * (g + 1))
                cg = cm[rows, gn]
                bg = bm[rows, gn]
                cb = _dot_nt(cg, bg)
                cb4 = jnp.concatenate([cb] * HEADS_PER_GROUP, axis=1)
                arow = jnp.concatenate([acum_t[h:h + 1, rows] for h in heads], axis=1)
                dtrow = jnp.concatenate([dt_t[h:h + 1, rows] for h in heads], axis=1)
                seg = a_x[rows, HEADS_PER_GROUP * Q * g:HEADS_PER_GROUP * Q * (g + 1)] - arow
                mp = cb4 * jnp.exp2(jnp.where(causal, seg, -jnp.inf)) * dtrow
                xg = xs[rows, gx]
                xg_b = xg.astype(_BF16)
                bd = jnp.where(blockdiag, jnp.concatenate([xg_b] * HEADS_PER_GROUP, axis=0),
                               jnp.zeros((), _BF16))
                y_diag = _dot(mp.astype(_BF16), bd)
                st = st_ref[s, g]
                y_off = _dot(cg, st.astype(_BF16)) * ea_x[rows, gx]
                y_groups.append(y_diag + y_off)
                xw = (xg * w_x[rows, gx]).astype(_BF16)
                st_ref[s, g] = st * ea_x[r0 + Q - 1:r0 + Q, gx] + _dot_tn(bg, xw)
                if fillers:
                    fillers.pop(0)()
            y_rows.append(jnp.concatenate(y_groups, axis=1))
    y = y_rows[0] if len(y_rows) == 1 else jnp.concatenate(y_rows, axis=0)

    while fillers:
        fillers.pop(0)()
    y = y + dsk_ref[...] * xs
    vb = y * jax.nn.silu(z_ref[...])
    yb = (vb * _rms_scale(vb) * nbw_ref[...]).astype(_BF16)

    out = outa_ref[...] + _dot(yb, wout_ref[D_A:D_A + D_B, :])
    res = per_seq(lambda s, rows: x[rows] + mod_ref[s, 2:3, :] * out[rows])
    y_ref[...] = (res * _rms_scale(res) * nf_ref[...]).reshape(S, Lb, D_MODEL)

    @pl.when(t == pl.num_programs(1) - 1)
    def _store_carried_state():
        for s in range(S):
            oa_ref[s] = ubuf[s, HIST - (CONV_A_W - 1):HIST, :]
            ob_ref[s] = xbuf[s, HIST - (CONV_B_W - 1):HIST, :]
            for g in range(GROUPS):
                os_ref[s, GROUP_W * g:GROUP_W * (g + 1), :] = st_ref[s, g].T


def _expand_matrix(lanes_per_head):
    e = np.zeros((LANES, HEADS * lanes_per_head), np.float32)
    for k in range(DT_COPIES):
        for h in range(HEADS):
            e[k * HEADS + h, h * lanes_per_head:(h + 1) * lanes_per_head] = 1.0
    return jnp.asarray(e, _BF16)


def _chunk_tri(rows, q):
    i = np.arange(rows)
    return jnp.asarray((i[:, None] // q == i[None, :] // q) & (i[None, :] <= i[:, None]), _BF16)


def _const_spec(shape):
    return pl.BlockSpec(shape, lambda i, j: (0,) * len(shape), pipeline_mode=pl.Buffered(1))


def _layer_call(x, mod3, sa, sb, ss, params, *, S, Lb, Q):
    ns, L, _ = x.shape
    R = S * Lb
    consts = (_expand_matrix(HEADDIM), _expand_matrix(Q), _chunk_tri(R, Q))
    seq3 = lambda i, j: (i, 0, 0)
    in_specs = [
        pl.BlockSpec((S, Lb, D_MODEL), lambda i, j: (i, j, 0)),
        pl.BlockSpec((S, 3, D_MODEL), seq3),
        pl.BlockSpec((S, CONV_A_W - 1, D_A), seq3),
        pl.BlockSpec((S, CONV_B_W - 1, D_XBC), seq3),
        pl.BlockSpec((S, HEADS * HEADDIM, D_STATE), seq3),
    ] + [_const_spec(p.shape) for p in params + consts]
    out_shape = (
        jax.ShapeDtypeStruct((ns, L, D_MODEL), _F32),
        jax.ShapeDtypeStruct((ns, CONV_A_W - 1, D_A), _F32),
        jax.ShapeDtypeStruct((ns, CONV_B_W - 1, D_XBC), _F32),
        jax.ShapeDtypeStruct((ns, HEADS * HEADDIM, D_STATE), _F32),
    )
    out_specs = (
        pl.BlockSpec((S, Lb, D_MODEL), lambda i, j: (i, j, 0)),
        pl.BlockSpec((S, CONV_A_W - 1, D_A), seq3),
        pl.BlockSpec((S, CONV_B_W - 1, D_XBC), seq3),
        pl.BlockSpec((S, HEADS * HEADDIM, D_STATE), seq3),
    )
    scratch = [
        pltpu.VMEM((S, HIST, D_A), _F32),
        pltpu.VMEM((S, HIST, D_XBC), _F32),
        pltpu.VMEM((S, GROUPS, D_STATE, GROUP_W), _F32),
        pltpu.VMEM((R, D_B), _F32),
        pltpu.VMEM((R, D_MODEL), _F32),
    ]
    return pl.pallas_call(
        functools.partial(_layer_kernel, S=S, Lb=Lb, Q=Q),
        grid=(ns // S, L // Lb),
        in_specs=in_specs,
        out_specs=out_specs,
        out_shape=out_shape,
        scratch_shapes=scratch,
        compiler_params=pltpu.CompilerParams(
            dimension_semantics=("arbitrary", "arbitrary"),
            vmem_limit_bytes=VMEM_LIMIT_BYTES),
        name=f"layer_s{S}_l{Lb}",
    )(x, mod3, sa, sb, ss, *params, *consts)


def _mod_call(c_all, w_mod, b_mod):
    rows = c_all.shape[0]
    n = w_mod.shape[1]
    bn = D_MODEL
    return pl.pallas_call(
        _mod_kernel,
        grid=(n // bn,),
        in_specs=[pl.BlockSpec((rows, D_MODEL), lambda j: (0, 0)),
                  pl.BlockSpec((D_MODEL, bn), lambda j: (0, j)),
                  pl.BlockSpec((1, bn), lambda j: (0, j))],
        out_specs=pl.BlockSpec((rows, bn), lambda j: (0, j)),
        out_shape=jax.ShapeDtypeStruct((rows, n), _F32),
        name="mod_proj",
    )(c_all, w_mod, b_mod)


def _pad_heads(v):
    return jnp.pad(jnp.tile(v, DT_COPIES), (0, DT_W - DT_COPIES * HEADS)).reshape(1, DT_W)


def kernel(x_prompt, x_sample, state_conv_a, state_conv_b, state_ssm, c_prompt, c_sample, w_mod, b_mod, norm_in_w, w_in, conv_a_w, norm_a_w, conv_b_w, conv_b_b, dt_bias, a_log, d_skip, norm_b_w, w_out, norm_f_w):
    depth = w_in.shape[0]
    assert depth == 1, "single-layer step"
    assert w_in.shape[2] == D_IN_PROJ
    nb, nd = x_prompt.shape[0], x_sample.shape[0]

    n_rows = nb + nd
    rows_pad = -(-n_rows // 16) * 16
    c_all = jnp.pad(jnp.concatenate([c_prompt, c_sample], axis=0), ((0, rows_pad - n_rows), (0, 0)))
    mod = _mod_call(c_all.astype(_BF16), w_mod[0].astype(_BF16), b_mod[0].reshape(1, -1))
    mod3 = mod[:n_rows].reshape(n_rows, 3, D_MODEL)

    w_in_b = w_in[0].astype(_BF16)
    w_dt = w_in_b[:, OFF_DT:]
    w_dt3 = jnp.concatenate([w_dt] * DT_COPIES + [jnp.zeros((D_MODEL, DT_W - DT_COPIES * HEADS), _BF16)], axis=1)
    params = (
        norm_in_w[0].reshape(1, D_MODEL), w_in_b, w_dt3, conv_a_w[0], norm_a_w[0].reshape(1, D_A),
        conv_b_w[0], conv_b_b[0].reshape(1, D_XBC), _pad_heads(dt_bias[0]), _pad_heads(a_log[0]),
        jnp.repeat(d_skip[0], HEADDIM).reshape(1, D_B), norm_b_w[0].reshape(1, D_B),
        w_out[0].astype(_BF16), norm_f_w.reshape(1, D_MODEL),
    )

    f32 = x_prompt.dtype
    za = jnp.zeros((nb, CONV_A_W - 1, D_A), f32)
    zb = jnp.zeros((nb, CONV_B_W - 1, D_XBC), f32)
    zs = jnp.zeros((nb, HEADS * HEADDIM, D_STATE), state_ssm.dtype)
    y_p, ca_p, cb_p, ss_p = _layer_call(x_prompt, mod3[:nb], za, zb, zs, params, S=1, Lb=512, Q=128)
    y_s, ca_s, cb_s, ss_s = _layer_call(
        x_sample, mod3[nb:], state_conv_a[0], state_conv_b[0],
        state_ssm[0].reshape(nd, HEADS * HEADDIM, D_STATE), params, S=4, Lb=x_sample.shape[1], Q=64)

    shp = lambda a: a.reshape(1, a.shape[0], HEADS, HEADDIM, D_STATE)
    return (y_p, y_s, ca_p[None], cb_p[None], shp(ss_p), ca_s[None], cb_s[None], shp(ss_s))
```

```python
import functools

import jax
import jax.numpy as jnp
import numpy as np
from jax import lax
from jax.experimental import pallas as pl
from jax.experimental.pallas import tpu as pltpu

D_MODEL = 1024
D_A = 1024
D_B = 1024
CONV_A_W = 3
CONV_B_W = 4
HEADS = 16
HEADDIM = 64
GROUPS = 4
HEADS_PER_GROUP = HEADS // GROUPS
D_STATE = 128
D_BC = GROUPS * D_STATE
D_XBC = D_B + 2 * D_BC
D_IN_PROJ = 4 * D_A + D_B + D_XBC + HEADS
GROUP_W = HEADS_PER_GROUP * HEADDIM
EPS = 1e-5
LOG2_E = 1.4426950408889634

LANES = 128
MXU_W = 256
HIST = 8
DT_COPIES = 3
DT_W = LANES

OFF_A = 0
OFF_Z = 4 * D_A
OFF_XBC = OFF_Z + D_B
OFF_DT = OFF_XBC + D_XBC

VMEM_LIMIT_BYTES = 56 * 1024 * 1024

_F32 = jnp.float32
_BF16 = jnp.bfloat16


def _dot(a, b):
    return jnp.dot(a, b, preferred_element_type=_F32)


def _dot_nt(a, b):
    return lax.dot_general(a, b, (((1,), (1,)), ((), ())), preferred_element_type=_F32)


def _dot_tn(a, b):
    return lax.dot_general(a, b, (((0,), (0,)), ((), ())), preferred_element_type=_F32)


def _rms_scale(v):
    return lax.rsqrt(jnp.mean(v * v, axis=-1, keepdims=True) + EPS)


def _split3(v):
    hi = v.astype(_BF16)
    r1 = v - hi.astype(_F32)
    mid = r1.astype(_BF16)
    lo = (r1 - mid.astype(_F32)).astype(_BF16)
    lane = lax.broadcasted_iota(jnp.int32, v.shape, 1)
    return jnp.where(lane < HEADS, hi, jnp.where(lane < 2 * HEADS, mid, lo))


def _mod_kernel(c_ref, w_ref, b_ref, o_ref):
    o_ref[...] = _dot(c_ref[...], w_ref[...]) + b_ref[...]


def _layer_kernel(x_ref, mod_ref, sa_ref, sb_ref, ss_ref,
                  nin_ref, win_ref, wdt_ref, caw_ref, naw_ref, cbw_ref, cbb_ref, dtb_ref, alog_ref,
                  dsk_ref, nbw_ref, wout_ref, nf_ref, e64_ref, eq_ref, tri_ref,
                  y_ref, oa_ref, ob_ref, os_ref,
                  ubuf, xbuf, st_ref, z_ref, outa_ref, *, S, Lb, Q):
    t = pl.program_id(1)
    R = S * Lb

    @pl.when(t == 0)
    def _load_carried_state():
        for s in range(S):
            for hist_ref, state_ref, width in ((ubuf, sa_ref, CONV_A_W), (xbuf, sb_ref, CONV_B_W)):
                hist_ref[s] = jnp.zeros(hist_ref.shape[1:], _F32)
                hist_ref[s, HIST - (width - 1):HIST, :] = state_ref[s]
            for g in range(GROUPS):
                st_ref[s, g] = ss_ref[s, GROUP_W * g:GROUP_W * (g + 1), :].T

    def per_seq(fn):
        parts = [fn(s, slice(s * Lb, (s + 1) * Lb)) for s in range(S)]
        return parts[0] if S == 1 else jnp.concatenate(parts, axis=0)

    x = x_ref[...].reshape(R, D_MODEL)

    xn = x * _rms_scale(x)
    hb = per_seq(lambda s, rows: xn[rows] * (nin_ref[...] * (1.0 + mod_ref[s, 1:2, :])) + mod_ref[s, 0:1, :]).astype(_BF16)

    def causal_conv(val, hist_ref, s, w_ref):
        width = w_ref.shape[0]
        ext = jnp.concatenate([hist_ref[s], val], axis=0)
        acc = val * w_ref[width - 1:width, :]
        for d in range(1, width):
            acc = acc + pltpu.roll(ext, d, axis=0)[HIST:HIST + Lb] * w_ref[width - 1 - d:width - d, :]
        hist_ref[s] = val[Lb - HIST:Lb]
        return acc

    xr = _dot(hb, win_ref[:, OFF_XBC:OFF_XBC + D_XBC])
    dtr = _dot(hb, wdt_ref[...])
    xbc = jax.nn.silu(per_seq(lambda s, rows: causal_conv(xr[rows], xbuf, s, cbw_ref)) + cbb_ref[...])
    xs = xbc[:, 0:D_B]
    bm = xbc[:, D_B:D_B + D_BC].astype(_BF16)
    cm = xbc[:, D_B + D_BC:D_XBC].astype(_BF16)

    pa = _dot(hb, win_ref[:, OFF_A:OFF_A + 4 * D_A])
    cb_all = {}
    for s in range(S):
        for c in range(Lb // Q):
            rows = slice(s * Lb + c * Q, s * Lb + (c + 1) * Q)
            for g in range(GROUPS):
                gn = slice(D_STATE * g, D_STATE * (g + 1))
                cb_all[s, c, g] = _dot_nt(cm[rows, gn], bm[rows, gn])
    dt = jax.nn.softplus(dtr + dtb_ref[...])
    da = dt * (-jnp.exp(alog_ref[...]))
    tri = tri_ref[...]
    da_hi = da.astype(_BF16)
    da_r1 = da - da_hi.astype(_F32)
    da_mid = da_r1.astype(_BF16)
    da_lo = (da_r1 - da_mid.astype(_F32)).astype(_BF16)
    acum = _dot(tri, da_hi) + _dot(tri, da_mid) + _dot(tri, da_lo)
    nchunks = R // Q
    alast = jnp.broadcast_to(acum.reshape(nchunks, Q, LANES)[:, Q - 1:Q, :],
                             (nchunks, Q, LANES)).reshape(R, LANES)
    ea_x = _dot(_split3(jnp.exp(acum)), e64_ref[...])
    w_x = _dot(_split3(dt * jnp.exp(alast - acum)), e64_ref[...])
    acum2 = acum * LOG2_E
    a_x = _dot(_split3(acum2), eq_ref[...])
    acum_t = acum2.T
    dt_t = dt.T

    b_gate = pa[:, 0:D_A]
    u = pa[:, D_A:2 * D_A] * pa[:, 2 * D_A:3 * D_A]
    g_a = pa[:, 3 * D_A:4 * D_A]
    va = b_gate * per_seq(lambda s, rows: causal_conv(u[rows], ubuf, s, caw_ref)) * jax.nn.silu(g_a)
    ya = (va * _rms_scale(va) * naw_ref[...]).astype(_BF16)

    def z_chunk(c):
        z_ref[:, c:c + MXU_W] = _dot(hb, win_ref[:, OFF_Z + c:OFF_Z + c + MXU_W])

    def outa_chunk(c):
        outa_ref[:, c:c + MXU_W] = _dot(ya, wout_ref[0:D_A, c:c + MXU_W])

    fillers = ([functools.partial(z_chunk, c) for c in range(0, D_B, MXU_W)]
               + [functools.partial(outa_chunk, c) for c in range(0, D_MODEL, MXU_W)])

    row_i = lax.broadcasted_iota(jnp.int32, (Q, HEADS_PER_GROUP * Q), 0)
    col_i = lax.broadcasted_iota(jnp.int32, (Q, HEADS_PER_GROUP * Q), 1)
    causal = row_i >= (col_i % Q)
    bd_row = lax.broadcasted_iota(jnp.int32, (HEADS_PER_GROUP * Q, GROUP_W), 0) // Q
    bd_col = lax.broadcasted_iota(jnp.int32, (HEADS_PER_GROUP * Q, GROUP_W), 1) // HEADDIM
    blockdiag = bd_row == bd_col

    y_rows = []
    for s in range(S):
        for c in range(Lb // Q):
            r0 = s * Lb + c * Q
            rows = slice(r0, r0 + Q)
            y_groups = []
            for g in range(GROUPS):
                gx = slice(GROUP_W * g, GROUP_W * (g + 1))
                gn = slice(D_STATE * g, D_STATE * (g + 1))
                heads = range(HEADS_PER_GROUP * g, HEADS_PER_GROUP * (g + 1))
                cg = cm[rows, gn]
                bg = bm[rows, gn]
                cb4 = jnp.concatenate([cb_all[s, c, g]] * HEADS_PER_GROUP, axis=1)
                arow = jnp.concatenate([acum_t[h:h + 1, rows] for h in heads], axis=1)
                dtrow = jnp.concatenate([dt_t[h:h + 1, rows] for h in heads], axis=1)
                seg = a_x[rows, HEADS_PER_GROUP * Q * g:HEADS_PER_GROUP * Q * (g + 1)] - arow
                mp = cb4 * jnp.exp2(jnp.where(causal, seg, -jnp.inf)) * dtrow
                xg = xs[rows, gx]
                xg_b = xg.astype(_BF16)
                bd = jnp.where(blockdiag, jnp.concatenate([xg_b] * HEADS_PER_GROUP, axis=0),
                               jnp.zeros((), _BF16))
                y_diag = _dot(mp.astype(_BF16), bd)
                st = st_ref[s, g]
                y_off = _dot(cg, st.astype(_BF16)) * ea_x[rows, gx]
                y_groups.append(y_diag + y_off)
                xw = (xg * w_x[rows, gx]).astype(_BF16)
                st_ref[s, g] = st * ea_x[r0 + Q - 1:r0 + Q, gx] + _dot_tn(bg, xw)
                if fillers:
                    fillers.pop(0)()
            y_rows.append(jnp.concatenate(y_groups, axis=1))
    y = y_rows[0] if len(y_rows) == 1 else jnp.concatenate(y_rows, axis=0)

    while fillers:
        fillers.pop(0)()
    y = y + dsk_ref[...] * xs
    vb = y * jax.nn.silu(z_ref[...])
    yb = (vb * _rms_scale(vb) * nbw_ref[...]).astype(_BF16)

    out = outa_ref[...] + _dot(yb, wout_ref[D_A:D_A + D_B, :])
    res = per_seq(lambda s, rows: x[rows] + mod_ref[s, 2:3, :] * out[rows])
    y_ref[...] = (res * _rms_scale(res) * nf_ref[...]).reshape(S, Lb, D_MODEL)

    @pl.when(t == pl.num_programs(1) - 1)
    def _store_carried_state():
        for s in range(S):
            oa_ref[s] = ubuf[s, HIST - (CONV_A_W - 1):HIST, :]
            ob_ref[s] = xbuf[s, HIST - (CONV_B_W - 1):HIST, :]
            for g in range(GROUPS):
                os_ref[s, GROUP_W * g:GROUP_W * (g + 1), :] = st_ref[s, g].T


def _expand_matrix(lanes_per_head):
    e = np.zeros((LANES, HEADS * lanes_per_head), np.float32)
    for k in range(DT_COPIES):
        for h in range(HEADS):
            e[k * HEADS + h, h * lanes_per_head:(h + 1) * lanes_per_head] = 1.0
    return jnp.asarray(e, _BF16)


def _chunk_tri(rows, q):
    i = np.arange(rows)
    return jnp.asarray((i[:, None] // q == i[None, :] // q) & (i[None, :] <= i[:, None]), _BF16)


def _const_spec(shape):
    return pl.BlockSpec(shape, lambda i, j: (0,) * len(shape), pipeline_mode=pl.Buffered(1))


def _layer_call(x, mod3, sa, sb, ss, params, *, S, Lb, Q):
    ns, L, _ = x.shape
    R = S * Lb
    consts = (_expand_matrix(HEADDIM), _expand_matrix(Q), _chunk_tri(R, Q))
    seq3 = lambda i, j: (i, 0, 0)
    in_specs = [
        pl.BlockSpec((S, Lb, D_MODEL), lambda i, j: (i, j, 0)),
        pl.BlockSpec((S, 3, D_MODEL), seq3),
        pl.BlockSpec((S, CONV_A_W - 1, D_A), seq3),
        pl.BlockSpec((S, CONV_B_W - 1, D_XBC), seq3),
        pl.BlockSpec((S, HEADS * HEADDIM, D_STATE), seq3),
    ] + [_const_spec(p.shape) for p in params + consts]
    out_shape = (
        jax.ShapeDtypeStruct((ns, L, D_MODEL), _F32),
        jax.ShapeDtypeStruct((ns, CONV_A_W - 1, D_A), _F32),
        jax.ShapeDtypeStruct((ns, CONV_B_W - 1, D_XBC), _F32),
        jax.ShapeDtypeStruct((ns, HEADS * HEADDIM, D_STATE), _F32),
    )
    out_specs = (
        pl.BlockSpec((S, Lb, D_MODEL), lambda i, j: (i, j, 0)),
        pl.BlockSpec((S, CONV_A_W - 1, D_A), seq3),
        pl.BlockSpec((S, CONV_B_W - 1, D_XBC), seq3),
        pl.BlockSpec((S, HEADS * HEADDIM, D_STATE), seq3),
    )
    scratch = [
        pltpu.VMEM((S, HIST, D_A), _F32),
        pltpu.VMEM((S, HIST, D_XBC), _F32),
        pltpu.VMEM((S, GROUPS, D_STATE, GROUP_W), _F32),
        pltpu.VMEM((R, D_B), _F32),
        pltpu.VMEM((R, D_MODEL), _F32),
    ]
    return pl.pallas_call(
        functools.partial(_layer_kernel, S=S, Lb=Lb, Q=Q),
        grid=(ns // S, L // Lb),
        in_specs=in_specs,
        out_specs=out_specs,
        out_shape=out_shape,
        scratch_shapes=scratch,
        compiler_params=pltpu.CompilerParams(
            dimension_semantics=("arbitrary", "arbitrary"),
            vmem_limit_bytes=VMEM_LIMIT_BYTES),
        name=f"layer_s{S}_l{Lb}",
    )(x, mod3, sa, sb, ss, *params, *consts)


def _mod_call(c_all, w_mod, b_mod):
    rows = c_all.shape[0]
    n = w_mod.shape[1]
    bn = D_MODEL
    return pl.pallas_call(
        _mod_kernel,
        grid=(n // bn,),
        in_specs=[pl.BlockSpec((rows, D_MODEL), lambda j: (0, 0)),
                  pl.BlockSpec((D_MODEL, bn), lambda j: (0, j)),
                  pl.BlockSpec((1, bn), lambda j: (0, j))],
        out_specs=pl.BlockSpec((rows, bn), lambda j: (0, j)),
        out_shape=jax.ShapeDtypeStruct((rows, n), _F32),
        name="mod_proj",
    )(c_all, w_mod, b_mod)


def _pad_heads(v):
    return jnp.pad(jnp.tile(v, DT_COPIES), (0, DT_W - DT_COPIES * HEADS)).reshape(1, DT_W)


def kernel(x_prompt, x_sample, state_conv_a, state_conv_b, state_ssm, c_prompt, c_sample, w_mod, b_mod, norm_in_w, w_in, conv_a_w, norm_a_w, conv_b_w, conv_b_b, dt_bias, a_log, d_skip, norm_b_w, w_out, norm_f_w):
    depth = w_in.shape[0]
    assert depth == 1, "single-layer step"
    assert w_in.shape[2] == D_IN_PROJ
    nb, nd = x_prompt.shape[0], x_sample.shape[0]

    n_rows = nb + nd
    rows_pad = -(-n_rows // 16) * 16
    c_all = jnp.pad(jnp.concatenate([c_prompt, c_sample], axis=0), ((0, rows_pad - n_rows), (0, 0)))
    mod = _mod_call(c_all.astype(_BF16), w_mod[0].astype(_BF16), b_mod[0].reshape(1, -1))
    mod3 = mod[:n_rows].reshape(n_rows, 3, D_MODEL)

    w_in_b = w_in[0].astype(_BF16)
    w_dt = w_in_b[:, OFF_DT:]
    w_dt3 = jnp.concatenate([w_dt] * DT_COPIES + [jnp.zeros((D_MODEL, DT_W - DT_COPIES * HEADS), _BF16)], axis=1)
    params = (
        norm_in_w[0].reshape(1, D_MODEL), w_in_b, w_dt3, conv_a_w[0], norm_a_w[0].reshape(1, D_A),
        conv_b_w[0], conv_b_b[0].reshape(1, D_XBC), _pad_heads(dt_bias[0]), _pad_heads(a_log[0]),
        jnp.repeat(d_skip[0], HEADDIM).reshape(1, D_B), norm_b_w[0].reshape(1, D_B),
        w_out[0].astype(_BF16), norm_f_w.reshape(1, D_MODEL),
    )

    f32 = x_prompt.dtype
    za = jnp.zeros((nb, CONV_A_W - 1, D_A), f32)
    zb = jnp.zeros((nb, CONV_B_W - 1, D_XBC), f32)
    zs = jnp.zeros((nb, HEADS * HEADDIM, D_STATE), state_ssm.dtype)
    y_p, ca_p, cb_p, ss_p = _layer_call(x_prompt, mod3[:nb], za, zb, zs, params, S=1, Lb=256, Q=128)
    y_s, ca_s, cb_s, ss_s = _layer_call(
        x_sample, mod3[nb:], state_conv_a[0], state_conv_b[0],
        state_ssm[0].reshape(nd, HEADS * HEADDIM, D_STATE), params, S=4, Lb=x_sample.shape[1], Q=64)

    shp = lambda a: a.reshape(1, a.shape[0], HEADS, HEADDIM, D_STATE)
    return (y_p, y_s, ca_p[None], cb_p[None], shp(ss_p), ca_s[None], cb_s[None], shp(ss_s))
```

```python
import functools

import jax
import jax.numpy as jnp
import numpy as np
from jax import lax
from jax.experimental import pallas as pl
from jax.experimental.pallas import tpu as pltpu

D_MODEL = 1024
D_A = 1024
D_B = 1024
CONV_A_W = 3
CONV_B_W = 4
HEADS = 16
HEADDIM = 64
GROUPS = 4
HEADS_PER_GROUP = HEADS // GROUPS
D_STATE = 128
D_BC = GROUPS * D_STATE
D_XBC = D_B + 2 * D_BC
D_IN_PROJ = 4 * D_A + D_B + D_XBC + HEADS
GROUP_W = HEADS_PER_GROUP * HEADDIM
EPS = 1e-5
LOG2_E = 1.4426950408889634

LANES = 128
MXU_W = 256
HIST = 8
DT_COPIES = 3
DT_W = LANES

OFF_A = 0
OFF_Z = 4 * D_A
OFF_XBC = OFF_Z + D_B
OFF_DT = OFF_XBC + D_XBC

VMEM_LIMIT_BYTES = 56 * 1024 * 1024

_F32 = jnp.float32
_BF16 = jnp.bfloat16


def _dot(a, b):
    return jnp.dot(a, b, preferred_element_type=_F32)


def _dot_nt(a, b):
    return lax.dot_general(a, b, (((1,), (1,)), ((), ())), preferred_element_type=_F32)


def _dot_tn(a, b):
    return lax.dot_general(a, b, (((0,), (0,)), ((), ())), preferred_element_type=_F32)


def _rms_scale(v):
    return lax.rsqrt(jnp.mean(v * v, axis=-1, keepdims=True) + EPS)


def _split3(v):
    hi = v.astype(_BF16)
    r1 = v - hi.astype(_F32)
    mid = r1.astype(_BF16)
    lo = (r1 - mid.astype(_F32)).astype(_BF16)
    lane = lax.broadcasted_iota(jnp.int32, v.shape, 1)
    return jnp.where(lane < HEADS, hi, jnp.where(lane < 2 * HEADS, mid, lo))


def _mod_kernel(c_ref, w_ref, b_ref, o_ref):
    o_ref[...] = _dot(c_ref[...], w_ref[...]) + b_ref[...]


def _layer_kernel(x_ref, mod_ref, sa_ref, sb_ref, ss_ref,
                  nin_ref, win_ref, wdt_ref, caw_ref, naw_ref, cbw_ref, cbb_ref, dtb_ref, alog_ref,
                  dsk_ref, nbw_ref, wout_ref, nf_ref, e64_ref, tri_ref,
                  y_ref, oa_ref, ob_ref, os_ref,
                  ubuf, xbuf, st_ref, z_ref, outa_ref, *, S, Lb, Q):
    t = pl.program_id(1)
    R = S * Lb

    @pl.when(t == 0)
    def _load_carried_state():
        for s in range(S):
            for hist_ref, state_ref, width in ((ubuf, sa_ref, CONV_A_W), (xbuf, sb_ref, CONV_B_W)):
                hist_ref[s] = jnp.zeros(hist_ref.shape[1:], _F32)
                hist_ref[s, HIST - (width - 1):HIST, :] = state_ref[s]
            for g in range(GROUPS):
                st_ref[s, g] = ss_ref[s, GROUP_W * g:GROUP_W * (g + 1), :].T

    def per_seq(fn):
        parts = [fn(s, slice(s * Lb, (s + 1) * Lb)) for s in range(S)]
        return parts[0] if S == 1 else jnp.concatenate(parts, axis=0)

    x = x_ref[...].reshape(R, D_MODEL)

    xn = x * _rms_scale(x)
    hb = per_seq(lambda s, rows: xn[rows] * (nin_ref[...] * (1.0 + mod_ref[s, 1:2, :])) + mod_ref[s, 0:1, :]).astype(_BF16)

    def causal_conv(val, hist_ref, s, w_ref):
        width = w_ref.shape[0]
        ext = jnp.concatenate([hist_ref[s], val], axis=0)
        acc = val * w_ref[width - 1:width, :]
        for d in range(1, width):
            acc = acc + pltpu.roll(ext, d, axis=0)[HIST:HIST + Lb] * w_ref[width - 1 - d:width - d, :]
        hist_ref[s] = val[Lb - HIST:Lb]
        return acc

    xr = _dot(hb, win_ref[:, OFF_XBC:OFF_XBC + D_XBC])
    dtr = _dot(hb, wdt_ref[...])
    xbc = jax.nn.silu(per_seq(lambda s, rows: causal_conv(xr[rows], xbuf, s, cbw_ref)) + cbb_ref[...])
    xs = xbc[:, 0:D_B]
    bm = xbc[:, D_B:D_B + D_BC].astype(_BF16)
    cm = xbc[:, D_B + D_BC:D_XBC].astype(_BF16)

    pa_ch = _dot(hb, win_ref[:, OFF_A + D_A:OFF_A + 3 * D_A])
    pa_b = _dot(hb, win_ref[:, OFF_A:OFF_A + D_A])
    pa_g = _dot(hb, win_ref[:, OFF_A + 3 * D_A:OFF_A + 4 * D_A])
    cb_all = {}
    for s in range(S):
        for c in range(Lb // Q):
            rows = slice(s * Lb + c * Q, s * Lb + (c + 1) * Q)
            for g in range(GROUPS):
                gn = slice(D_STATE * g, D_STATE * (g + 1))
                cb_all[s, c, g] = _dot_nt(cm[rows, gn], bm[rows, gn])
    dt = jax.nn.softplus(dtr + dtb_ref[...])
    da = dt * (-jnp.exp(alog_ref[...]))
    tri = tri_ref[...]
    da_hi = da.astype(_BF16)
    da_r1 = da - da_hi.astype(_F32)
    da_mid = da_r1.astype(_BF16)
    da_lo = (da_r1 - da_mid.astype(_F32)).astype(_BF16)
    acum = _dot(tri, da_hi) + _dot(tri, da_mid) + _dot(tri, da_lo)
    nchunks = R // Q
    alast = jnp.broadcast_to(acum.reshape(nchunks, Q, LANES)[:, Q - 1:Q, :],
                             (nchunks, Q, LANES)).reshape(R, LANES)
    ea_x = _dot(_split3(jnp.exp(acum)), e64_ref[...])
    w_x = _dot(_split3(dt * jnp.exp(alast - acum)), e64_ref[...])
    acum2 = acum * LOG2_E
    acum_t = acum2.T
    dt_t = dt.T

    b_gate = pa_b
    u = pa_ch[:, 0:D_A] * pa_ch[:, D_A:2 * D_A]
    g_a = pa_g
    va = b_gate * per_seq(lambda s, rows: causal_conv(u[rows], ubuf, s, caw_ref)) * jax.nn.silu(g_a)
    ya = (va * _rms_scale(va) * naw_ref[...]).astype(_BF16)

    def z_chunk(c):
        z_ref[:, c:c + MXU_W] = _dot(hb, win_ref[:, OFF_Z + c:OFF_Z + c + MXU_W])

    def outa_chunk(c):
        outa_ref[:, c:c + MXU_W] = _dot(ya, wout_ref[0:D_A, c:c + MXU_W])

    fillers = ([functools.partial(z_chunk, c) for c in range(0, D_B, MXU_W)]
               + [functools.partial(outa_chunk, c) for c in range(0, D_MODEL, MXU_W)])

    row_i = lax.broadcasted_iota(jnp.int32, (Q, HEADS_PER_GROUP * Q), 0)
    col_i = lax.broadcasted_iota(jnp.int32, (Q, HEADS_PER_GROUP * Q), 1)
    causal = row_i >= (col_i % Q)
    bd_row = lax.broadcasted_iota(jnp.int32, (HEADS_PER_GROUP * Q, GROUP_W), 0) // Q
    bd_col = lax.broadcasted_iota(jnp.int32, (HEADS_PER_GROUP * Q, GROUP_W), 1) // HEADDIM
    blockdiag = bd_row == bd_col

    y_rows = []
    for s in range(S):
        for c in range(Lb // Q):
            r0 = s * Lb + c * Q
            rows = slice(r0, r0 + Q)
            y_groups = []
            for g in range(GROUPS):
                gx = slice(GROUP_W * g, GROUP_W * (g + 1))
                gn = slice(D_STATE * g, D_STATE * (g + 1))
                heads = range(HEADS_PER_GROUP * g, HEADS_PER_GROUP * (g + 1))
                cg = cm[rows, gn]
                bg = bm[rows, gn]
                cb4 = jnp.concatenate([cb_all[s, c, g]] * HEADS_PER_GROUP, axis=1)
                arow = jnp.concatenate([acum_t[h:h + 1, rows] for h in heads], axis=1)
                dtrow = jnp.concatenate([dt_t[h:h + 1, rows] for h in heads], axis=1)
                acol = jnp.concatenate([jnp.broadcast_to(acum2[rows, h:h + 1], (Q, Q)) for h in heads], axis=1)
                seg = acol - arow
                mp = cb4 * jnp.exp2(jnp.where(causal, seg, -jnp.inf)) * dtrow
                xg = xs[rows, gx]
                xg_b = xg.astype(_BF16)
                bd = jnp.where(blockdiag, jnp.concatenate([xg_b] * HEADS_PER_GROUP, axis=0),
                               jnp.zeros((), _BF16))
                y_diag = _dot(mp.astype(_BF16), bd)
                st = st_ref[s, g]
                y_off = _dot(cg, st.astype(_BF16)) * ea_x[rows, gx]
                y_groups.append(y_diag + y_off)
                xw = (xg * w_x[rows, gx]).astype(_BF16)
                st_ref[s, g] = st * ea_x[r0 + Q - 1:r0 + Q, gx] + _dot_tn(bg, xw)
                if fillers:
                    fillers.pop(0)()
            y_rows.append(jnp.concatenate(y_groups, axis=1))
    y = y_rows[0] if len(y_rows) == 1 else jnp.concatenate(y_rows, axis=0)

    while fillers:
        fillers.pop(0)()
    y = y + dsk_ref[...] * xs
    vb = y * jax.nn.silu(z_ref[...])
    yb = (vb * _rms_scale(vb) * nbw_ref[...]).astype(_BF16)

    out = outa_ref[...] + _dot(yb, wout_ref[D_A:D_A + D_B, :])
    res = per_seq(lambda s, rows: x[rows] + mod_ref[s, 2:3, :] * out[rows])
    y_ref[...] = (res * _rms_scale(res) * nf_ref[...]).reshape(S, Lb, D_MODEL)

    @pl.when(t == pl.num_programs(1) - 1)
    def _store_carried_state():
        for s in range(S):
            oa_ref[s] = ubuf[s, HIST - (CONV_A_W - 1):HIST, :]
            ob_ref[s] = xbuf[s, HIST - (CONV_B_W - 1):HIST, :]
            for g in range(GROUPS):
                os_ref[s, GROUP_W * g:GROUP_W * (g + 1), :] = st_ref[s, g].T


def _expand_matrix(lanes_per_head):
    e = np.zeros((LANES, HEADS * lanes_per_head), np.float32)
    for k in range(DT_COPIES):
        for h in range(HEADS):
            e[k * HEADS + h, h * lanes_per_head:(h + 1) * lanes_per_head] = 1.0
    return jnp.asarray(e, _BF16)


def _chunk_tri(rows, q):
    i = np.arange(rows)
    return jnp.asarray((i[:, None] // q == i[None, :] // q) & (i[None, :] <= i[:, None]), _BF16)


def _const_spec(shape):
    return pl.BlockSpec(shape, lambda i, j: (0,) * len(shape), pipeline_mode=pl.Buffered(1))


def _layer_call(x, mod3, sa, sb, ss, params, *, S, Lb, Q):
    ns, L, _ = x.shape
    R = S * Lb
    consts = (_expand_matrix(HEADDIM), _chunk_tri(R, Q))
    seq3 = lambda i, j: (i, 0, 0)
    in_specs = [
        pl.BlockSpec((S, Lb, D_MODEL), lambda i, j: (i, j, 0)),
        pl.BlockSpec((S, 3, D_MODEL), seq3),
        pl.BlockSpec((S, CONV_A_W - 1, D_A), seq3),
        pl.BlockSpec((S, CONV_B_W - 1, D_XBC), seq3),
        pl.BlockSpec((S, HEADS * HEADDIM, D_STATE), seq3),
    ] + [_const_spec(p.shape) for p in params + consts]
    out_shape = (
        jax.ShapeDtypeStruct((ns, L, D_MODEL), _F32),
        jax.ShapeDtypeStruct((ns, CONV_A_W - 1, D_A), _F32),
        jax.ShapeDtypeStruct((ns, CONV_B_W - 1, D_XBC), _F32),
        jax.ShapeDtypeStruct((ns, HEADS * HEADDIM, D_STATE), _F32),
    )
    out_specs = (
        pl.BlockSpec((S, Lb, D_MODEL), lambda i, j: (i, j, 0)),
        pl.BlockSpec((S, CONV_A_W - 1, D_A), seq3),
        pl.BlockSpec((S, CONV_B_W - 1, D_XBC), seq3),
        pl.BlockSpec((S, HEADS * HEADDIM, D_STATE), seq3),
    )
    scratch = [
        pltpu.VMEM((S, HIST, D_A), _F32),
        pltpu.VMEM((S, HIST, D_XBC), _F32),
        pltpu.VMEM((S, GROUPS, D_STATE, GROUP_W), _F32),
        pltpu.VMEM((R, D_B), _F32),
        pltpu.VMEM((R, D_MODEL), _F32),
    ]
    return pl.pallas_call(
        functools.partial(_layer_kernel, S=S, Lb=Lb, Q=Q),
        grid=(ns // S, L // Lb),
        in_specs=in_specs,
        out_specs=out_specs,
        out_shape=out_shape,
        scratch_shapes=scratch,
        compiler_params=pltpu.CompilerParams(
            dimension_semantics=("arbitrary", "arbitrary"),
            vmem_limit_bytes=VMEM_LIMIT_BYTES),
        name=f"layer_s{S}_l{Lb}",
    )(x, mod3, sa, sb, ss, *params, *consts)


def _mod_call(c_all, w_mod, b_mod):
    rows = c_all.shape[0]
    n = w_mod.shape[1]
    bn = D_MODEL
    return pl.pallas_call(
        _mod_kernel,
        grid=(n // bn,),
        in_specs=[pl.BlockSpec((rows, D_MODEL), lambda j: (0, 0)),
                  pl.BlockSpec((D_MODEL, bn), lambda j: (0, j)),
                  pl.BlockSpec((1, bn), lambda j: (0, j))],
        out_specs=pl.BlockSpec((rows, bn), lambda j: (0, j)),
        out_shape=jax.ShapeDtypeStruct((rows, n), _F32),
        name="mod_proj",
    )(c_all, w_mod, b_mod)


def _pad_heads(v):
    return jnp.pad(jnp.tile(v, DT_COPIES), (0, DT_W - DT_COPIES * HEADS)).reshape(1, DT_W)


def kernel(x_prompt, x_sample, state_conv_a, state_conv_b, state_ssm, c_prompt, c_sample, w_mod, b_mod, norm_in_w, w_in, conv_a_w, norm_a_w, conv_b_w, conv_b_b, dt_bias, a_log, d_skip, norm_b_w, w_out, norm_f_w):
    depth = w_in.shape[0]
    assert depth == 1, "single-layer step"
    assert w_in.shape[2] == D_IN_PROJ
    nb, nd = x_prompt.shape[0], x_sample.shape[0]

    n_rows = nb + nd
    rows_pad = -(-n_rows // 16) * 16
    c_all = jnp.pad(jnp.concatenate([c_prompt, c_sample], axis=0), ((0, rows_pad - n_rows), (0, 0)))
    mod = _mod_call(c_all.astype(_BF16), w_mod[0].astype(_BF16), b_mod[0].reshape(1, -1))
    mod3 = mod[:n_rows].reshape(n_rows, 3, D_MODEL)

    w_in_b = w_in[0].astype(_BF16)
    w_dt = w_in_b[:, OFF_DT:]
    w_dt3 = jnp.concatenate([w_dt] * DT_COPIES + [jnp.zeros((D_MODEL, DT_W - DT_COPIES * HEADS), _BF16)], axis=1)
    params = (
        norm_in_w[0].reshape(1, D_MODEL), w_in_b, w_dt3, conv_a_w[0], norm_a_w[0].reshape(1, D_A),
        conv_b_w[0], conv_b_b[0].reshape(1, D_XBC), _pad_heads(dt_bias[0]), _pad_heads(a_log[0]),
        jnp.repeat(d_skip[0], HEADDIM).reshape(1, D_B), norm_b_w[0].reshape(1, D_B),
        w_out[0].astype(_BF16), norm_f_w.reshape(1, D_MODEL),
    )

    f32 = x_prompt.dtype
    za = jnp.zeros((nb, CONV_A_W - 1, D_A), f32)
    zb = jnp.zeros((nb, CONV_B_W - 1, D_XBC), f32)
    zs = jnp.zeros((nb, HEADS * HEADDIM, D_STATE), state_ssm.dtype)
    y_p, ca_p, cb_p, ss_p = _layer_call(x_prompt, mod3[:nb], za, zb, zs, params, S=1, Lb=256, Q=128)
    y_s, ca_s, cb_s, ss_s = _layer_call(
        x_sample, mod3[nb:], state_conv_a[0], state_conv_b[0],
        state_ssm[0].reshape(nd, HEADS * HEADDIM, D_STATE), params, S=4, Lb=x_sample.shape[1], Q=64)

    shp = lambda a: a.reshape(1, a.shape[0], HEADS, HEADDIM, D_STATE)
    return (y_p, y_s, ca_p[None], cb_p[None], shp(ss_p), ca_s[None], cb_s[None], shp(ss_s))
```

```python
import functools

import jax
import jax.numpy as jnp
import numpy as np
from jax import lax
from jax.experimental import pallas as pl
from jax.experimental.pallas import tpu as pltpu

D_MODEL = 1024
D_A = 1024
D_B = 1024
CONV_A_W = 3
CONV_B_W = 4
HEADS = 16
HEADDIM = 64
GROUPS = 4
HEADS_PER_GROUP = HEADS // GROUPS
D_STATE = 128
D_BC = GROUPS * D_STATE
D_XBC = D_B + 2 * D_BC
D_IN_PROJ = 4 * D_A + D_B + D_XBC + HEADS
GROUP_W = HEADS_PER_GROUP * HEADDIM
EPS = 1e-5
LOG2_E = 1.4426950408889634

LANES = 128
MXU_W = 256
HIST = 8
DT_COPIES = 3
DT_W = LANES
ROW_PAD = LANES

OFF_A = 0
OFF_Z = 4 * D_A
OFF_XBC = OFF_Z + D_B
OFF_DT = OFF_XBC + D_XBC

VMEM_LIMIT_BYTES = 56 * 1024 * 1024

_F32 = jnp.float32
_BF16 = jnp.bfloat16


def _dot(a, b):
    return jnp.dot(a, b, preferred_element_type=_F32)


def _dot_nt(a, b):
    return lax.dot_general(a, b, (((1,), (1,)), ((), ())), preferred_element_type=_F32)


def _dot_tn(a, b):
    return lax.dot_general(a, b, (((0,), (0,)), ((), ())), preferred_element_type=_F32)


def _rms_scale(v):
    return lax.rsqrt(jnp.mean(v * v, axis=-1, keepdims=True) + EPS)


def _split3(v):
    hi = v.astype(_BF16)
    r1 = v - hi.astype(_F32)
    mid = r1.astype(_BF16)
    lo = (r1 - mid.astype(_F32)).astype(_BF16)
    lane = lax.broadcasted_iota(jnp.int32, v.shape, 1)
    return jnp.where(lane < HEADS, hi, jnp.where(lane < 2 * HEADS, mid, lo))


def _mod_kernel(c_ref, w_ref, b_ref, o_ref):
    o_ref[...] = _dot(c_ref[...], w_ref[...].astype(_BF16)) + b_ref[...]


def _layer_kernel(x_ref, mod_ref, sa_ref, sb_ref, ss_ref,
                  nin_ref, win_ref, wdt_ref, caw_ref, naw_ref, cbw_ref, cbb_ref, dtb_ref, alog_ref,
                  dsk_ref, nbw_ref, wout_ref, nf_ref, e64_ref, tri_ref,
                  y_ref, oa_ref, ob_ref, os_ref,
                  ubuf, xbuf, st_ref, z_ref, outa_ref, *, S, Lb, Q):
    t = pl.program_id(1)
    R = S * Lb

    @pl.when(t == 0)
    def _load_carried_state():
        for s in range(S):
            for hist_ref, state_ref, width in ((ubuf, sa_ref, CONV_A_W), (xbuf, sb_ref, CONV_B_W)):
                hist_ref[s] = jnp.zeros(hist_ref.shape[1:], _F32)
                hist_ref[s, HIST - (width - 1):HIST, :] = state_ref[s]
            for g in range(GROUPS):
                st_ref[s, g] = ss_ref[s, GROUP_W * g:GROUP_W * (g + 1), :].T

    def per_seq(fn):
        parts = [fn(s, slice(s * Lb, (s + 1) * Lb)) for s in range(S)]
        return parts[0] if S == 1 else jnp.concatenate(parts, axis=0)

    x = x_ref[...].reshape(R, D_MODEL)

    xn = x * _rms_scale(x)
    hb = per_seq(lambda s, rows: xn[rows] * (nin_ref[...] * (1.0 + mod_ref[s, 1:2, :])) + mod_ref[s, 0:1, :]).astype(_BF16)

    def causal_conv(val, hist_ref, s, w_ref):
        width = w_ref.shape[0]
        ext = jnp.concatenate([hist_ref[s], val], axis=0)
        acc = val * w_ref[width - 1:width, :]
        for d in range(1, width):
            acc = acc + pltpu.roll(ext, d, axis=0)[HIST:HIST + Lb] * w_ref[width - 1 - d:width - d, :]
        hist_ref[s] = val[Lb - HIST:Lb]
        return acc

    xr = _dot(hb, win_ref[:, OFF_XBC:OFF_XBC + D_XBC])
    dtr = _dot(hb, wdt_ref[...])
    xbc = jax.nn.silu(per_seq(lambda s, rows: causal_conv(xr[rows], xbuf, s, cbw_ref)) + cbb_ref[...])
    xs = xbc[:, 0:D_B]
    bm = xbc[:, D_B:D_B + D_BC].astype(_BF16)
    cm = xbc[:, D_B + D_BC:D_XBC].astype(_BF16)

    pa_ch = _dot(hb, win_ref[:, OFF_A + D_A:OFF_A + 3 * D_A])
    pa_b = _dot(hb, win_ref[:, OFF_A:OFF_A + D_A])
    pa_g = _dot(hb, win_ref[:, OFF_A + 3 * D_A:OFF_A + 4 * D_A])
    cb_all = {}
    for s in range(S):
        for c in range(Lb // Q):
            rows = slice(s * Lb + c * Q, s * Lb + (c + 1) * Q)
            for g in range(GROUPS):
                gn = slice(D_STATE * g, D_STATE * (g + 1))
                cb_all[s, c, g] = _dot_nt(cm[rows, gn], bm[rows, gn])
    dt = jax.nn.softplus(dtr + dtb_ref[...])
    da = dt * (-jnp.exp(alog_ref[...]))
    tri = tri_ref[...]
    da_hi = da.astype(_BF16)
    da_r1 = da - da_hi.astype(_F32)
    da_mid = da_r1.astype(_BF16)
    da_lo = (da_r1 - da_mid.astype(_F32)).astype(_BF16)
    acum = _dot(tri, da_hi) + _dot(tri, da_mid) + _dot(tri, da_lo)
    nchunks = R // Q
    alast = jnp.broadcast_to(acum.reshape(nchunks, Q, LANES)[:, Q - 1:Q, :],
                             (nchunks, Q, LANES)).reshape(R, LANES)
    ea_x = _dot(_split3(jnp.exp(acum)), e64_ref[...])
    w_x = _dot(_split3(dt * jnp.exp(alast - acum)), e64_ref[...])
    acum2 = acum * LOG2_E
    acum_t = acum2.T
    dt_t = dt.T

    b_gate = pa_b
    u = pa_ch[:, 0:D_A] * pa_ch[:, D_A:2 * D_A]
    g_a = pa_g
    va = b_gate * per_seq(lambda s, rows: causal_conv(u[rows], ubuf, s, caw_ref)) * jax.nn.silu(g_a)
    ya = (va * _rms_scale(va) * naw_ref[...]).astype(_BF16)

    def z_chunk(c):
        z_ref[:, c:c + MXU_W] = _dot(hb, win_ref[:, OFF_Z + c:OFF_Z + c + MXU_W])

    def outa_chunk(c):
        outa_ref[:, c:c + MXU_W] = _dot(ya, wout_ref[0:D_A, c:c + MXU_W])

    fillers = ([functools.partial(z_chunk, c) for c in range(0, D_B, MXU_W)]
               + [functools.partial(outa_chunk, c) for c in range(0, D_MODEL, MXU_W)])

    row_i = lax.broadcasted_iota(jnp.int32, (Q, HEADS_PER_GROUP * Q), 0)
    col_i = lax.broadcasted_iota(jnp.int32, (Q, HEADS_PER_GROUP * Q), 1)
    causal = row_i >= (col_i % Q)
    bd_row = lax.broadcasted_iota(jnp.int32, (HEADS_PER_GROUP * Q, GROUP_W), 0) // Q
    bd_col = lax.broadcasted_iota(jnp.int32, (HEADS_PER_GROUP * Q, GROUP_W), 1) // HEADDIM
    blockdiag = bd_row == bd_col

    y_rows = []
    for s in range(S):
        for c in range(Lb // Q):
            r0 = s * Lb + c * Q
            rows = slice(r0, r0 + Q)
            y_groups = []
            for g in range(GROUPS):
                gx = slice(GROUP_W * g, GROUP_W * (g + 1))
                gn = slice(D_STATE * g, D_STATE * (g + 1))
                heads = range(HEADS_PER_GROUP * g, HEADS_PER_GROUP * (g + 1))
                cg = cm[rows, gn]
                bg = bm[rows, gn]
                cb4 = jnp.concatenate([cb_all[s, c, g]] * HEADS_PER_GROUP, axis=1)
                arow = jnp.concatenate([acum_t[h:h + 1, rows] for h in heads], axis=1)
                dtrow = jnp.concatenate([dt_t[h:h + 1, rows] for h in heads], axis=1)
                acol = jnp.concatenate([jnp.broadcast_to(acum2[rows, h:h + 1], (Q, Q)) for h in heads], axis=1)
                seg = acol - arow
                mp = cb4 * jnp.exp2(jnp.where(causal, seg, -jnp.inf)) * dtrow
                xg = xs[rows, gx]
                xg_b = xg.astype(_BF16)
                bd = jnp.where(blockdiag, jnp.concatenate([xg_b] * HEADS_PER_GROUP, axis=0),
                               jnp.zeros((), _BF16))
                y_diag = _dot(mp.astype(_BF16), bd)
                st = st_ref[s, g]
                y_off = _dot(cg, st.astype(_BF16)) * ea_x[rows, gx]
                y_groups.append(y_diag + y_off)
                xw = (xg * w_x[rows, gx]).astype(_BF16)
                st_ref[s, g] = st * ea_x[r0 + Q - 1:r0 + Q, gx] + _dot_tn(bg, xw)
                if fillers:
                    fillers.pop(0)()
            y_rows.append(jnp.concatenate(y_groups, axis=1))
    y = y_rows[0] if len(y_rows) == 1 else jnp.concatenate(y_rows, axis=0)

    while fillers:
        fillers.pop(0)()
    y = y + dsk_ref[...] * xs
    vb = y * jax.nn.silu(z_ref[:, 0:D_B])
    yb = (vb * _rms_scale(vb) * nbw_ref[...]).astype(_BF16)

    out = outa_ref[:, 0:D_MODEL] + _dot(yb, wout_ref[D_A:D_A + D_B, :])
    x_res = x_ref[...].reshape(R, D_MODEL)
    res = per_seq(lambda s, rows: x_res[rows] + mod_ref[s, 2:3, :] * out[rows])
    y_ref[...] = (res * _rms_scale(res) * nf_ref[...]).reshape(S, Lb, D_MODEL)

    @pl.when(t == pl.num_programs(1) - 1)
    def _store_carried_state():
        for s in range(S):
            oa_ref[s] = ubuf[s, HIST - (CONV_A_W - 1):HIST, :]
            ob_ref[s] = xbuf[s, HIST - (CONV_B_W - 1):HIST, :]
            for g in range(GROUPS):
                os_ref[s, GROUP_W * g:GROUP_W * (g + 1), :] = st_ref[s, g].T


def _expand_matrix(lanes_per_head):
    e = np.zeros((LANES, HEADS * lanes_per_head), np.float32)
    for k in range(DT_COPIES):
        for h in range(HEADS):
            e[k * HEADS + h, h * lanes_per_head:(h + 1) * lanes_per_head] = 1.0
    return jnp.asarray(e, _BF16)


def _chunk_tri(rows, q):
    i = np.arange(rows)
    return jnp.asarray((i[:, None] // q == i[None, :] // q) & (i[None, :] <= i[:, None]), _BF16)


def _const_spec(shape):
    return pl.BlockSpec(shape, lambda i, j: (0,) * len(shape), pipeline_mode=pl.Buffered(1))


def _layer_call(x, mod3, sa, sb, ss, params, *, S, Lb, Q):
    ns, L, _ = x.shape
    R = S * Lb
    consts = (_expand_matrix(HEADDIM), _chunk_tri(R, Q))
    seq3 = lambda i, j: (i, 0, 0)
    in_specs = [
        pl.BlockSpec((S, Lb, D_MODEL), lambda i, j: (i, j, 0)),
        pl.BlockSpec((S, 3, D_MODEL), seq3),
        pl.BlockSpec((S, CONV_A_W - 1, D_A), seq3),
        pl.BlockSpec((S, CONV_B_W - 1, D_XBC), seq3),
        pl.BlockSpec((S, HEADS * HEADDIM, D_STATE), seq3),
    ] + [_const_spec(p.shape) for p in params + consts]
    out_shape = (
        jax.ShapeDtypeStruct((ns, L, D_MODEL), _F32),
        jax.ShapeDtypeStruct((ns, CONV_A_W - 1, D_A), _F32),
        jax.ShapeDtypeStruct((ns, CONV_B_W - 1, D_XBC), _F32),
        jax.ShapeDtypeStruct((ns, HEADS * HEADDIM, D_STATE), _F32),
    )
    out_specs = (
        pl.BlockSpec((S, Lb, D_MODEL), lambda i, j: (i, j, 0)),
        pl.BlockSpec((S, CONV_A_W - 1, D_A), seq3),
        pl.BlockSpec((S, CONV_B_W - 1, D_XBC), seq3),
        pl.BlockSpec((S, HEADS * HEADDIM, D_STATE), seq3),
    )
    scratch = [
        pltpu.VMEM((S, HIST, D_A), _F32),
        pltpu.VMEM((S, HIST, D_XBC), _F32),
        pltpu.VMEM((S, GROUPS, D_STATE, GROUP_W), _F32),
        pltpu.VMEM((R, D_B + ROW_PAD), _F32),
        pltpu.VMEM((R, D_MODEL + ROW_PAD), _F32),
    ]
    return pl.pallas_call(
        functools.partial(_layer_kernel, S=S, Lb=Lb, Q=Q),
        grid=(ns // S, L // Lb),
        in_specs=in_specs,
        out_specs=out_specs,
        out_shape=out_shape,
        scratch_shapes=scratch,
        compiler_params=pltpu.CompilerParams(
            dimension_semantics=("arbitrary", "arbitrary"),
            vmem_limit_bytes=VMEM_LIMIT_BYTES),
        name=f"layer_s{S}_l{Lb}",
    )(x, mod3, sa, sb, ss, *params, *consts)


def _mod_call(c_all, w_mod, b_mod):
    rows = c_all.shape[0]
    n = w_mod.shape[1]
    bn = D_MODEL
    return pl.pallas_call(
        _mod_kernel,
        grid=(n // bn,),
        in_specs=[pl.BlockSpec((rows, D_MODEL), lambda j: (0, 0)),
                  pl.BlockSpec((D_MODEL, bn), lambda j: (0, j)),
                  pl.BlockSpec((1, bn), lambda j: (0, j))],
        out_specs=pl.BlockSpec((rows, bn), lambda j: (0, j)),
        out_shape=jax.ShapeDtypeStruct((rows, n), _F32),
        name="mod_proj",
    )(c_all, w_mod, b_mod)


def _pad_heads(v):
    return jnp.pad(jnp.tile(v, DT_COPIES), (0, DT_W - DT_COPIES * HEADS)).reshape(1, DT_W)


def kernel(x_prompt, x_sample, state_conv_a, state_conv_b, state_ssm, c_prompt, c_sample, w_mod, b_mod, norm_in_w, w_in, conv_a_w, norm_a_w, conv_b_w, conv_b_b, dt_bias, a_log, d_skip, norm_b_w, w_out, norm_f_w):
    depth = w_in.shape[0]
    assert depth == 1, "single-layer step"
    assert w_in.shape[2] == D_IN_PROJ
    nb, nd = x_prompt.shape[0], x_sample.shape[0]

    n_rows = nb + nd
    rows_pad = -(-n_rows // 16) * 16
    c_all = jnp.pad(jnp.concatenate([c_prompt, c_sample], axis=0), ((0, rows_pad - n_rows), (0, 0)))
    mod = _mod_call(c_all.astype(_BF16), w_mod[0], b_mod[0].reshape(1, -1))
    mod3 = mod[:n_rows].reshape(n_rows, 3, D_MODEL)

    w_in_b = w_in[0].astype(_BF16)
    w_dt = w_in_b[:, OFF_DT:]
    w_dt3 = jnp.concatenate([w_dt] * DT_COPIES + [jnp.zeros((D_MODEL, DT_W - DT_COPIES * HEADS), _BF16)], axis=1)
    params = (
        norm_in_w[0].reshape(1, D_MODEL), w_in_b, w_dt3, conv_a_w[0], norm_a_w[0].reshape(1, D_A),
        conv_b_w[0], conv_b_b[0].reshape(1, D_XBC), _pad_heads(dt_bias[0]), _pad_heads(a_log[0]),
        jnp.repeat(d_skip[0], HEADDIM).reshape(1, D_B), norm_b_w[0].reshape(1, D_B),
        w_out[0].astype(_BF16), norm_f_w.reshape(1, D_MODEL),
    )

    f32 = x_prompt.dtype
    za = jnp.zeros((nb, CONV_A_W - 1, D_A), f32)
    zb = jnp.zeros((nb, CONV_B_W - 1, D_XBC), f32)
    zs = jnp.zeros((nb, HEADS * HEADDIM, D_STATE), state_ssm.dtype)
    y_p, ca_p, cb_p, ss_p = _layer_call(x_prompt, mod3[:nb], za, zb, zs, params, S=1, Lb=256, Q=128)
    y_s, ca_s, cb_s, ss_s = _layer_call(
        x_sample, mod3[nb:], state_conv_a[0], state_conv_b[0],
        state_ssm[0].reshape(nd, HEADS * HEADDIM, D_STATE), params, S=4, Lb=x_sample.shape[1], Q=64)

    shp = lambda a: a.reshape(1, a.shape[0], HEADS, HEADDIM, D_STATE)
    return (y_p, y_s, ca_p[None], cb_p[None], shp(ss_p), ca_s[None], cb_s[None], shp(ss_s))
```

```python
import functools

import jax
import jax.numpy as jnp
import numpy as np
from jax import lax
from jax.experimental import pallas as pl
from jax.experimental.pallas import tpu as pltpu

D_MODEL = 1024
D_A = 1024
D_B = 1024
CONV_A_W = 3
CONV_B_W = 4
HEADS = 16
HEADDIM = 64
GROUPS = 4
HEADS_PER_GROUP = HEADS // GROUPS
D_STATE = 128
D_BC = GROUPS * D_STATE
D_XBC = D_B + 2 * D_BC
D_IN_PROJ = 4 * D_A + D_B + D_XBC + HEADS
GROUP_W = HEADS_PER_GROUP * HEADDIM
EPS = 1e-5
LOG2_E = 1.4426950408889634

LANES = 128
MXU_W = 256
HIST = 8
DT_COPIES = 3
DT_W = LANES
ROW_PAD = LANES

OFF_A = 0
OFF_Z = 4 * D_A
OFF_XBC = OFF_Z + D_B
OFF_DT = OFF_XBC + D_XBC

VMEM_LIMIT_BYTES = 56 * 1024 * 1024

PROMPT_BLOCK_ROWS = 256
PROMPT_SSD_CHUNK = 128
SAMPLE_SEQS_PER_STEP = 4

_F32 = jnp.float32
_BF16 = jnp.bfloat16


def _dot(a, b):
    return jnp.dot(a, b, preferred_element_type=_F32)


def _dot_nt(a, b):
    return lax.dot_general(a, b, (((1,), (1,)), ((), ())), preferred_element_type=_F32)


def _dot_tn(a, b):
    return lax.dot_general(a, b, (((0,), (0,)), ((), ())), preferred_element_type=_F32)


def _rms_scale(v):
    return lax.rsqrt(jnp.mean(v * v, axis=-1, keepdims=True) + EPS)


def _split3(v):
    hi = v.astype(_BF16)
    r1 = v - hi.astype(_F32)
    mid = r1.astype(_BF16)
    lo = (r1 - mid.astype(_F32)).astype(_BF16)
    lane = lax.broadcasted_iota(jnp.int32, v.shape, 1)
    return jnp.where(lane < HEADS, hi, jnp.where(lane < 2 * HEADS, mid, lo))


def _mod_kernel(c_ref, w_ref, b_ref, o_ref):
    o_ref[...] = _dot(c_ref[...], w_ref[...].astype(_BF16)) + b_ref[...]


def _layer_kernel(x_ref, mod_ref, sa_ref, sb_ref, ss_ref,
                  nin_ref, win_ref, wdt_ref, caw_ref, naw_ref, cbw_ref, cbb_ref, dtb_ref, alog_ref,
                  dsk_ref, nbw_ref, wout_ref, nf_ref, e64_ref, tri_ref,
                  y_ref, oa_ref, ob_ref, os_ref,
                  ubuf, xbuf, st_ref, z_ref, outa_ref, *, S, Lb, Q):
    t = pl.program_id(1)
    R = S * Lb

    @pl.when(t == 0)
    def _load_carried_state():
        for s in range(S):
            for hist_ref, state_ref, width in ((ubuf, sa_ref, CONV_A_W), (xbuf, sb_ref, CONV_B_W)):
                hist_ref[s] = jnp.zeros(hist_ref.shape[1:], _F32)
                hist_ref[s, HIST - (width - 1):HIST, :] = state_ref[s]
            for g in range(GROUPS):
                st_ref[s, g] = ss_ref[s, GROUP_W * g:GROUP_W * (g + 1), :].T

    def per_seq(fn):
        parts = [fn(s, slice(s * Lb, (s + 1) * Lb)) for s in range(S)]
        return parts[0] if S == 1 else jnp.concatenate(parts, axis=0)

    x = x_ref[...].reshape(R, D_MODEL)

    xn = x * _rms_scale(x)
    hb = per_seq(lambda s, rows: xn[rows] * (nin_ref[...] * (1.0 + mod_ref[s, 1:2, :])) + mod_ref[s, 0:1, :]).astype(_BF16)

    def causal_conv(val, hist_ref, s, w_ref):
        width = w_ref.shape[0]
        ext = jnp.concatenate([hist_ref[s], val], axis=0)
        acc = val * w_ref[width - 1:width, :]
        for d in range(1, width):
            acc = acc + pltpu.roll(ext, d, axis=0)[HIST:HIST + Lb] * w_ref[width - 1 - d:width - d, :]
        hist_ref[s] = val[Lb - HIST:Lb]
        return acc

    xr = _dot(hb, win_ref[:, OFF_XBC:OFF_XBC + D_XBC])
    dtr = _dot(hb, wdt_ref[...])
    xbc = jax.nn.silu(per_seq(lambda s, rows: causal_conv(xr[rows], xbuf, s, cbw_ref)) + cbb_ref[...])
    xs = xbc[:, 0:D_B]
    bm = xbc[:, D_B:D_B + D_BC].astype(_BF16)
    cm = xbc[:, D_B + D_BC:D_XBC].astype(_BF16)

    pa_ch = _dot(hb, win_ref[:, OFF_A + D_A:OFF_A + 3 * D_A])
    pa_b = _dot(hb, win_ref[:, OFF_A:OFF_A + D_A])
    pa_g = _dot(hb, win_ref[:, OFF_A + 3 * D_A:OFF_A + 4 * D_A])
    cb_all = {}
    for s in range(S):
        for c in range(Lb // Q):
            rows = slice(s * Lb + c * Q, s * Lb + (c + 1) * Q)
            for g in range(GROUPS):
                gn = slice(D_STATE * g, D_STATE * (g + 1))
                cb_all[s, c, g] = _dot_nt(cm[rows, gn], bm[rows, gn])
    dt = jax.nn.softplus(dtr + dtb_ref[...])
    da = dt * (-jnp.exp(alog_ref[...]))
    tri = tri_ref[...]
    da_hi = da.astype(_BF16)
    da_r1 = da - da_hi.astype(_F32)
    da_mid = da_r1.astype(_BF16)
    da_lo = (da_r1 - da_mid.astype(_F32)).astype(_BF16)
    acum = _dot(tri, da_hi) + _dot(tri, da_mid) + _dot(tri, da_lo)
    nchunks = R // Q
    alast = jnp.broadcast_to(acum.reshape(nchunks, Q, LANES)[:, Q - 1:Q, :],
                             (nchunks, Q, LANES)).reshape(R, LANES)
    ea_x = _dot(_split3(jnp.exp(acum)), e64_ref[...])
    w_x = _dot(_split3(dt * jnp.exp(alast - acum)), e64_ref[...])
    acum2 = acum * LOG2_E
    acum_t = acum2.T
    dt_t = dt.T

    b_gate = pa_b
    u = pa_ch[:, 0:D_A] * pa_ch[:, D_A:2 * D_A]
    g_a = pa_g
    va = b_gate * per_seq(lambda s, rows: causal_conv(u[rows], ubuf, s, caw_ref)) * jax.nn.silu(g_a)
    ya = (va * _rms_scale(va) * naw_ref[...]).astype(_BF16)

    def z_chunk(c):
        z_ref[:, c:c + MXU_W] = _dot(hb, win_ref[:, OFF_Z + c:OFF_Z + c + MXU_W])

    def outa_chunk(c):
        cols = slice(c, c + MXU_W)
        out_a = _dot(ya, wout_ref[0:D_A, cols])
        outa_ref[:, cols] = per_seq(lambda s, rows: x_ref[s, :, cols] + mod_ref[s, 2:3, cols] * out_a[rows])

    fillers = ([functools.partial(z_chunk, c) for c in range(0, D_B, MXU_W)]
               + [functools.partial(outa_chunk, c) for c in range(0, D_MODEL, MXU_W)])

    row_i = lax.broadcasted_iota(jnp.int32, (Q, HEADS_PER_GROUP * Q), 0)
    col_i = lax.broadcasted_iota(jnp.int32, (Q, HEADS_PER_GROUP * Q), 1)
    causal = row_i >= (col_i % Q)
    bd_row = lax.broadcasted_iota(jnp.int32, (HEADS_PER_GROUP * Q, GROUP_W), 0) // Q
    bd_col = lax.broadcasted_iota(jnp.int32, (HEADS_PER_GROUP * Q, GROUP_W), 1) // HEADDIM
    blockdiag = bd_row == bd_col

    y_rows = []
    for s in range(S):
        for c in range(Lb // Q):
            r0 = s * Lb + c * Q
            rows = slice(r0, r0 + Q)
            y_groups = []
            for g in range(GROUPS):
                gx = slice(GROUP_W * g, GROUP_W * (g + 1))
                gn = slice(D_STATE * g, D_STATE * (g + 1))
                heads = range(HEADS_PER_GROUP * g, HEADS_PER_GROUP * (g + 1))
                cg = cm[rows, gn]
                bg = bm[rows, gn]
                cb4 = jnp.concatenate([cb_all[s, c, g]] * HEADS_PER_GROUP, axis=1)
                arow = jnp.concatenate([acum_t[h:h + 1, rows] for h in heads], axis=1)
                dtrow = jnp.concatenate([dt_t[h:h + 1, rows] for h in heads], axis=1)
                acol = jnp.concatenate([jnp.broadcast_to(acum2[rows, h:h + 1], (Q, Q)) for h in heads], axis=1)
                seg = acol - arow
                mp = cb4 * jnp.exp2(jnp.where(causal, seg, -jnp.inf)) * dtrow
                xg = xs[rows, gx]
                xg_b = xg.astype(_BF16)
                bd = jnp.where(blockdiag, jnp.concatenate([xg_b] * HEADS_PER_GROUP, axis=0),
                               jnp.zeros((), _BF16))
                y_diag = _dot(mp.astype(_BF16), bd)
                st = st_ref[s, g]
                y_off = _dot(cg, st.astype(_BF16)) * ea_x[rows, gx]
                y_groups.append(y_diag + y_off)
                xw = (xg * w_x[rows, gx]).astype(_BF16)
                st_ref[s, g] = st * ea_x[r0 + Q - 1:r0 + Q, gx] + _dot_tn(bg, xw)
                if fillers:
                    fillers.pop(0)()
            y_rows.append(jnp.concatenate(y_groups, axis=1))
    y = y_rows[0] if len(y_rows) == 1 else jnp.concatenate(y_rows, axis=0)

    while fillers:
        fillers.pop(0)()
    y = y + dsk_ref[...] * xs
    vb = y * jax.nn.silu(z_ref[:, 0:D_B])
    yb = (vb * _rms_scale(vb) * nbw_ref[...]).astype(_BF16)

    out_b = _dot(yb, wout_ref[D_A:D_A + D_B, :])
    res = outa_ref[:, 0:D_MODEL] + per_seq(lambda s, rows: mod_ref[s, 2:3, :] * out_b[rows])
    y_ref[...] = (res * _rms_scale(res) * nf_ref[...]).reshape(S, Lb, D_MODEL)

    @pl.when(t == pl.num_programs(1) - 1)
    def _store_carried_state():
        for s in range(S):
            oa_ref[s] = ubuf[s, HIST - (CONV_A_W - 1):HIST, :]
            ob_ref[s] = xbuf[s, HIST - (CONV_B_W - 1):HIST, :]
            for g in range(GROUPS):
                os_ref[s, GROUP_W * g:GROUP_W * (g + 1), :] = st_ref[s, g].T


def _expand_matrix(lanes_per_head):
    e = np.zeros((LANES, HEADS * lanes_per_head), np.float32)
    for k in range(DT_COPIES):
        for h in range(HEADS):
            e[k * HEADS + h, h * lanes_per_head:(h + 1) * lanes_per_head] = 1.0
    return jnp.asarray(e, _BF16)


def _chunk_tri(rows, q):
    i = np.arange(rows)
    return jnp.asarray((i[:, None] // q == i[None, :] // q) & (i[None, :] <= i[:, None]), _BF16)


def _const_spec(shape):
    return pl.BlockSpec(shape, lambda i, j: (0,) * len(shape), pipeline_mode=pl.Buffered(1))


def _layer_call(x, mod3, sa, sb, ss, params, *, S, Lb, Q):
    ns, L, _ = x.shape
    R = S * Lb
    consts = (_expand_matrix(HEADDIM), _chunk_tri(R, Q))
    seq3 = lambda i, j: (i, 0, 0)
    in_specs = [
        pl.BlockSpec((S, Lb, D_MODEL), lambda i, j: (i, j, 0)),
        pl.BlockSpec((S, 3, D_MODEL), seq3),
        pl.BlockSpec((S, CONV_A_W - 1, D_A), seq3),
        pl.BlockSpec((S, CONV_B_W - 1, D_XBC), seq3),
        pl.BlockSpec((S, HEADS * HEADDIM, D_STATE), seq3),
    ] + [_const_spec(p.shape) for p in params + consts]
    out_shape = (
        jax.ShapeDtypeStruct((ns, L, D_MODEL), _F32),
        jax.ShapeDtypeStruct((ns, CONV_A_W - 1, D_A), _F32),
        jax.ShapeDtypeStruct((ns, CONV_B_W - 1, D_XBC), _F32),
        jax.ShapeDtypeStruct((ns, HEADS * HEADDIM, D_STATE), _F32),
    )
    out_specs = (
        pl.BlockSpec((S, Lb, D_MODEL), lambda i, j: (i, j, 0)),
        pl.BlockSpec((S, CONV_A_W - 1, D_A), seq3),
        pl.BlockSpec((S, CONV_B_W - 1, D_XBC), seq3),
        pl.BlockSpec((S, HEADS * HEADDIM, D_STATE), seq3),
    )
    scratch = [
        pltpu.VMEM((S, HIST, D_A), _F32),
        pltpu.VMEM((S, HIST, D_XBC), _F32),
        pltpu.VMEM((S, GROUPS, D_STATE, GROUP_W), _F32),
        pltpu.VMEM((R, D_B + ROW_PAD), _F32),
        pltpu.VMEM((R, D_MODEL + ROW_PAD), _F32),
    ]
    return pl.pallas_call(
        functools.partial(_layer_kernel, S=S, Lb=Lb, Q=Q),
        grid=(ns // S, L // Lb),
        in_specs=in_specs,
        out_specs=out_specs,
        out_shape=out_shape,
        scratch_shapes=scratch,
        compiler_params=pltpu.CompilerParams(
            dimension_semantics=("arbitrary", "arbitrary"),
            vmem_limit_bytes=VMEM_LIMIT_BYTES),
        name=f"layer_s{S}_l{Lb}",
    )(x, mod3, sa, sb, ss, *params, *consts)


def _mod_call(c_all, w_mod, b_mod):
    rows = c_all.shape[0]
    n = w_mod.shape[1]
    bn = D_MODEL
    return pl.pallas_call(
        _mod_kernel,
        grid=(n // bn,),
        in_specs=[pl.BlockSpec((rows, D_MODEL), lambda j: (0, 0)),
                  pl.BlockSpec((D_MODEL, bn), lambda j: (0, j)),
                  pl.BlockSpec((1, bn), lambda j: (0, j))],
        out_specs=pl.BlockSpec((rows, bn), lambda j: (0, j)),
        out_shape=jax.ShapeDtypeStruct((rows, n), _F32),
        name="mod_proj",
    )(c_all, w_mod, b_mod)


def _pad_heads(v):
    return jnp.pad(jnp.tile(v, DT_COPIES), (0, DT_W - DT_COPIES * HEADS)).reshape(1, DT_W)


def kernel(x_prompt, x_sample, state_conv_a, state_conv_b, state_ssm, c_prompt, c_sample, w_mod, b_mod, norm_in_w, w_in, conv_a_w, norm_a_w, conv_b_w, conv_b_b, dt_bias, a_log, d_skip, norm_b_w, w_out, norm_f_w):
    depth = w_in.shape[0]
    assert depth == 1, "single-layer step"
    assert w_in.shape[2] == D_IN_PROJ
    nb, nd = x_prompt.shape[0], x_sample.shape[0]

    n_rows = nb + nd
    rows_pad = -(-n_rows // 16) * 16
    c_all = jnp.pad(jnp.concatenate([c_prompt, c_sample], axis=0), ((0, rows_pad - n_rows), (0, 0)))
    mod = _mod_call(c_all.astype(_BF16), w_mod[0], b_mod[0].reshape(1, -1))
    mod3 = mod[:n_rows].reshape(n_rows, 3, D_MODEL)

    w_in_b = w_in[0].astype(_BF16)
    w_dt = w_in_b[:, OFF_DT:]
    w_dt3 = jnp.concatenate([w_dt] * DT_COPIES + [jnp.zeros((D_MODEL, DT_W - DT_COPIES * HEADS), _BF16)], axis=1)
    params = (
        norm_in_w[0].reshape(1, D_MODEL), w_in_b, w_dt3, conv_a_w[0], norm_a_w[0].reshape(1, D_A),
        conv_b_w[0], conv_b_b[0].reshape(1, D_XBC), _pad_heads(dt_bias[0]), _pad_heads(a_log[0]),
        jnp.repeat(d_skip[0], HEADDIM).reshape(1, D_B), norm_b_w[0].reshape(1, D_B),
        w_out[0].astype(_BF16), norm_f_w.reshape(1, D_MODEL),
    )

    f32 = x_prompt.dtype
    za = jnp.zeros((nb, CONV_A_W - 1, D_A), f32)
    zb = jnp.zeros((nb, CONV_B_W - 1, D_XBC), f32)
    zs = jnp.zeros((nb, HEADS * HEADDIM, D_STATE), state_ssm.dtype)
    y_p, ca_p, cb_p, ss_p = _layer_call(x_prompt, mod3[:nb], za, zb, zs, params,
                                        S=1, Lb=PROMPT_BLOCK_ROWS, Q=PROMPT_SSD_CHUNK)
    y_s, ca_s, cb_s, ss_s = _layer_call(
        x_sample, mod3[nb:], state_conv_a[0], state_conv_b[0],
        state_ssm[0].reshape(nd, HEADS * HEADDIM, D_STATE), params,
        S=SAMPLE_SEQS_PER_STEP, Lb=x_sample.shape[1], Q=x_sample.shape[1])

    shp = lambda a: a.reshape(1, a.shape[0], HEADS, HEADDIM, D_STATE)
    return (y_p, y_s, ca_p[None], cb_p[None], shp(ss_p), ca_s[None], cb_s[None], shp(ss_s))
```

```python
import functools

import jax
import jax.numpy as jnp
import numpy as np
from jax import lax
from jax.experimental import pallas as pl
from jax.experimental.pallas import tpu as pltpu

D_MODEL = 1024
D_A = 1024
D_B = 1024
CONV_A_W = 3
CONV_B_W = 4
HEADS = 16
HEADDIM = 64
GROUPS = 4
HEADS_PER_GROUP = HEADS // GROUPS
D_STATE = 128
D_BC = GROUPS * D_STATE
D_XBC = D_B + 2 * D_BC
D_IN_PROJ = 4 * D_A + D_B + D_XBC + HEADS
GROUP_W = HEADS_PER_GROUP * HEADDIM
EPS = 1e-5
LOG2_E = 1.4426950408889634

LANES = 128
MXU_W = 256
HIST = 8
DT_COPIES = 3
DT_W = LANES
ROW_PAD = LANES

OFF_A = 0
OFF_Z = 4 * D_A
OFF_XBC = OFF_Z + D_B
OFF_DT = OFF_XBC + D_XBC

VMEM_LIMIT_BYTES = 56 * 1024 * 1024

PROMPT_BLOCK_ROWS = 256
PROMPT_SSD_CHUNK = 128
SAMPLE_SEQS_PER_STEP = 4

_F32 = jnp.float32
_BF16 = jnp.bfloat16


def _dot(a, b):
    return jnp.dot(a, b, preferred_element_type=_F32)


def _dot_nt(a, b):
    return lax.dot_general(a, b, (((1,), (1,)), ((), ())), preferred_element_type=_F32)


def _dot_tn(a, b):
    return lax.dot_general(a, b, (((0,), (0,)), ((), ())), preferred_element_type=_F32)


def _rms_scale(v):
    return lax.rsqrt(jnp.mean(v * v, axis=-1, keepdims=True) + EPS)


def _split3(v):
    hi = v.astype(_BF16)
    r1 = v - hi.astype(_F32)
    mid = r1.astype(_BF16)
    lo = (r1 - mid.astype(_F32)).astype(_BF16)
    lane = lax.broadcasted_iota(jnp.int32, v.shape, 1)
    return jnp.where(lane < HEADS, hi, jnp.where(lane < 2 * HEADS, mid, lo))


def _mod_kernel(c_ref, w_ref, b_ref, o_ref):
    o_ref[...] = _dot(c_ref[...], w_ref[...].astype(_BF16)) + b_ref[...]


def _layer_kernel(x_ref, mod_ref, sa_ref, sb_ref, ss_ref,
                  nin_ref, win_ref, wdt_ref, caw_ref, naw_ref, cbw_ref, cbb_ref, dtb_ref, alog_ref,
                  dsk_ref, nbw_ref, wout_ref, nf_ref, e64_ref, tri_ref,
                  y_ref, oa_ref, ob_ref, os_ref,
                  ubuf, xbuf, st_ref, z_ref, outa_ref, *, S, Lb, Q):
    t = pl.program_id(1)
    R = S * Lb

    @pl.when(t == 0)
    def _load_carried_state():
        for s in range(S):
            for hist_ref, state_ref, width in ((ubuf, sa_ref, CONV_A_W), (xbuf, sb_ref, CONV_B_W)):
                hist_ref[s] = jnp.zeros(hist_ref.shape[1:], _F32)
                hist_ref[s, HIST - (width - 1):HIST, :] = state_ref[s]
            for g in range(GROUPS):
                st_ref[s, g] = ss_ref[s, GROUP_W * g:GROUP_W * (g + 1), :].T

    def per_seq(fn):
        parts = [fn(s, slice(s * Lb, (s + 1) * Lb)) for s in range(S)]
        return parts[0] if S == 1 else jnp.concatenate(parts, axis=0)

    x = x_ref[...].reshape(R, D_MODEL)

    xn = x * _rms_scale(x)
    hb = per_seq(lambda s, rows: xn[rows] * (nin_ref[...] * (1.0 + mod_ref[s, 1:2, :])) + mod_ref[s, 0:1, :]).astype(_BF16)

    def causal_conv(val, hist_ref, s, w_ref):
        width = w_ref.shape[0]
        ext = jnp.concatenate([hist_ref[s], val], axis=0)
        acc = val * w_ref[width - 1:width, :]
        for d in range(1, width):
            acc = acc + pltpu.roll(ext, d, axis=0)[HIST:HIST + Lb] * w_ref[width - 1 - d:width - d, :]
        hist_ref[s] = val[Lb - HIST:Lb]
        return acc

    xr = _dot(hb, win_ref[:, OFF_XBC:OFF_XBC + D_XBC])
    dtr = _dot(hb, wdt_ref[...])
    xbc = jax.nn.silu(per_seq(lambda s, rows: causal_conv(xr[rows], xbuf, s, cbw_ref)) + cbb_ref[...])
    xs = xbc[:, 0:D_B]
    bm = xbc[:, D_B:D_B + D_BC].astype(_BF16)
    cm = xbc[:, D_B + D_BC:D_XBC].astype(_BF16)

    pa_ch = _dot(hb, win_ref[:, OFF_A + D_A:OFF_A + 3 * D_A])
    pa_b = _dot(hb, win_ref[:, OFF_A:OFF_A + D_A])
    pa_g = _dot(hb, win_ref[:, OFF_A + 3 * D_A:OFF_A + 4 * D_A])
    cb_all = {}
    for s in range(S):
        for c in range(Lb // Q):
            rows = slice(s * Lb + c * Q, s * Lb + (c + 1) * Q)
            for g in range(GROUPS):
                gn = slice(D_STATE * g, D_STATE * (g + 1))
                cb_all[s, c, g] = _dot_nt(cm[rows, gn], bm[rows, gn])
    dt = jax.nn.softplus(dtr + dtb_ref[...])
    da = dt * (-jnp.exp(alog_ref[...]))
    tri = tri_ref[...]
    da_hi = da.astype(_BF16)
    da_r1 = da - da_hi.astype(_F32)
    da_mid = da_r1.astype(_BF16)
    da_lo = (da_r1 - da_mid.astype(_F32)).astype(_BF16)
    acum = _dot(tri, da_hi) + _dot(tri, da_mid) + _dot(tri, da_lo)
    nchunks = R // Q
    alast = jnp.broadcast_to(acum.reshape(nchunks, Q, LANES)[:, Q - 1:Q, :],
                             (nchunks, Q, LANES)).reshape(R, LANES)
    ea_x = _dot(_split3(jnp.exp(acum)), e64_ref[...])
    w_x = _dot(_split3(dt * jnp.exp(alast - acum)), e64_ref[...])
    acum2 = acum * LOG2_E
    acum_t = acum2.T
    dt_t = dt.T

    b_gate = pa_b
    u = pa_ch[:, 0:D_A] * pa_ch[:, D_A:2 * D_A]
    g_a = pa_g
    va = b_gate * per_seq(lambda s, rows: causal_conv(u[rows], ubuf, s, caw_ref)) * jax.nn.silu(g_a)
    ya = (va * _rms_scale(va) * naw_ref[...]).astype(_BF16)

    def z_chunk(c):
        z_ref[:, c:c + MXU_W] = jax.nn.silu(_dot(hb, win_ref[:, OFF_Z + c:OFF_Z + c + MXU_W]))

    def outa_chunk(c):
        cols = slice(c, c + MXU_W)
        out_a = _dot(ya, wout_ref[0:D_A, cols])
        outa_ref[:, cols] = per_seq(lambda s, rows: x_ref[s, :, cols] + mod_ref[s, 2:3, cols] * out_a[rows])

    fillers = ([functools.partial(z_chunk, c) for c in range(0, D_B, MXU_W)]
               + [functools.partial(outa_chunk, c) for c in range(0, D_MODEL, MXU_W)])

    row_i = lax.broadcasted_iota(jnp.int32, (Q, HEADS_PER_GROUP * Q), 0)
    col_i = lax.broadcasted_iota(jnp.int32, (Q, HEADS_PER_GROUP * Q), 1)
    causal = row_i >= (col_i % Q)
    bd_row = lax.broadcasted_iota(jnp.int32, (HEADS_PER_GROUP * Q, GROUP_W), 0) // Q
    bd_col = lax.broadcasted_iota(jnp.int32, (HEADS_PER_GROUP * Q, GROUP_W), 1) // HEADDIM
    blockdiag = bd_row == bd_col

    y_rows = []
    for s in range(S):
        for c in range(Lb // Q):
            r0 = s * Lb + c * Q
            rows = slice(r0, r0 + Q)
            y_groups = []
            for g in range(GROUPS):
                if fillers:
                    fillers.pop(0)()
                gx = slice(GROUP_W * g, GROUP_W * (g + 1))
                gn = slice(D_STATE * g, D_STATE * (g + 1))
                heads = range(HEADS_PER_GROUP * g, HEADS_PER_GROUP * (g + 1))
                cg = cm[rows, gn]
                bg = bm[rows, gn]
                cb4 = jnp.concatenate([cb_all[s, c, g]] * HEADS_PER_GROUP, axis=1)
                arow = jnp.concatenate([acum_t[h:h + 1, rows] for h in heads], axis=1)
                dtrow = jnp.concatenate([dt_t[h:h + 1, rows] for h in heads], axis=1)
                acol = jnp.concatenate([jnp.broadcast_to(acum2[rows, h:h + 1], (Q, Q)) for h in heads], axis=1)
                seg = acol - arow
                mp = cb4 * jnp.exp2(jnp.where(causal, seg, -jnp.inf)) * dtrow
                xg = xs[rows, gx]
                xg_b = xg.astype(_BF16)
                bd = jnp.where(blockdiag, jnp.concatenate([xg_b] * HEADS_PER_GROUP, axis=0),
                               jnp.zeros((), _BF16))
                y_diag = _dot(mp.astype(_BF16), bd)
                st = st_ref[s, g]
                y_off = _dot(cg, st.astype(_BF16)) * ea_x[rows, gx]
                y_groups.append(y_diag + y_off + dsk_ref[:, gx] * xg)
                xw = (xg * w_x[rows, gx]).astype(_BF16)
                st_ref[s, g] = st * ea_x[r0 + Q - 1:r0 + Q, gx] + _dot_tn(bg, xw)
            y_rows.append(jnp.concatenate(y_groups, axis=1))
    y = y_rows[0] if len(y_rows) == 1 else jnp.concatenate(y_rows, axis=0)

    while fillers:
        fillers.pop(0)()
    vb = y * z_ref[:, 0:D_B]
    yb = (vb * _rms_scale(vb) * nbw_ref[...]).astype(_BF16)

    out_b = _dot(yb, wout_ref[D_A:D_A + D_B, :])
    res = outa_ref[:, 0:D_MODEL] + per_seq(lambda s, rows: mod_ref[s, 2:3, :] * out_b[rows])
    y_ref[...] = (res * _rms_scale(res) * nf_ref[...]).reshape(S, Lb, D_MODEL)

    @pl.when(t == pl.num_programs(1) - 1)
    def _store_carried_state():
        for s in range(S):
            oa_ref[s] = ubuf[s, HIST - (CONV_A_W - 1):HIST, :]
            ob_ref[s] = xbuf[s, HIST - (CONV_B_W - 1):HIST, :]
            for g in range(GROUPS):
                os_ref[s, GROUP_W * g:GROUP_W * (g + 1), :] = st_ref[s, g].T


def _expand_matrix(lanes_per_head):
    e = np.zeros((LANES, HEADS * lanes_per_head), np.float32)
    for k in range(DT_COPIES):
        for h in range(HEADS):
            e[k * HEADS + h, h * lanes_per_head:(h + 1) * lanes_per_head] = 1.0
    return jnp.asarray(e, _BF16)


def _chunk_tri(rows, q):
    i = np.arange(rows)
    return jnp.asarray((i[:, None] // q == i[None, :] // q) & (i[None, :] <= i[:, None]), _BF16)


def _const_spec(shape):
    return pl.BlockSpec(shape, lambda i, j: (0,) * len(shape), pipeline_mode=pl.Buffered(1))


def _layer_call(x, mod3, sa, sb, ss, params, *, S, Lb, Q):
    ns, L, _ = x.shape
    R = S * Lb
    consts = (_expand_matrix(HEADDIM), _chunk_tri(R, Q))
    seq3 = lambda i, j: (i, 0, 0)
    in_specs = [
        pl.BlockSpec((S, Lb, D_MODEL), lambda i, j: (i, j, 0)),
        pl.BlockSpec((S, 3, D_MODEL), seq3),
        pl.BlockSpec((S, CONV_A_W - 1, D_A), seq3),
        pl.BlockSpec((S, CONV_B_W - 1, D_XBC), seq3),
        pl.BlockSpec((S, HEADS * HEADDIM, D_STATE), seq3),
    ] + [_const_spec(p.shape) for p in params + consts]
    out_shape = (
        jax.ShapeDtypeStruct((ns, L, D_MODEL), _F32),
        jax.ShapeDtypeStruct((ns, CONV_A_W - 1, D_A), _F32),
        jax.ShapeDtypeStruct((ns, CONV_B_W - 1, D_XBC), _F32),
        jax.ShapeDtypeStruct((ns, HEADS * HEADDIM, D_STATE), _F32),
    )
    out_specs = (
        pl.BlockSpec((S, Lb, D_MODEL), lambda i, j: (i, j, 0)),
        pl.BlockSpec((S, CONV_A_W - 1, D_A), seq3),
        pl.BlockSpec((S, CONV_B_W - 1, D_XBC), seq3),
        pl.BlockSpec((S, HEADS * HEADDIM, D_STATE), seq3),
    )
    scratch = [
        pltpu.VMEM((S, HIST, D_A), _F32),
        pltpu.VMEM((S, HIST, D_XBC), _F32),
        pltpu.VMEM((S, GROUPS, D_STATE, GROUP_W), _F32),
        pltpu.VMEM((R, D_B + ROW_PAD), _F32),
        pltpu.VMEM((R, D_MODEL + ROW_PAD), _F32),
    ]
    return pl.pallas_call(
        functools.partial(_layer_kernel, S=S, Lb=Lb, Q=Q),
        grid=(ns // S, L // Lb),
        in_specs=in_specs,
        out_specs=out_specs,
        out_shape=out_shape,
        scratch_shapes=scratch,
        compiler_params=pltpu.CompilerParams(
            dimension_semantics=("arbitrary", "arbitrary"),
            vmem_limit_bytes=VMEM_LIMIT_BYTES),
        name=f"layer_s{S}_l{Lb}",
    )(x, mod3, sa, sb, ss, *params, *consts)


def _mod_call(c_all, w_mod, b_mod):
    rows = c_all.shape[0]
    n = w_mod.shape[1]
    bn = D_MODEL
    return pl.pallas_call(
        _mod_kernel,
        grid=(n // bn,),
        in_specs=[pl.BlockSpec((rows, D_MODEL), lambda j: (0, 0)),
                  pl.BlockSpec((D_MODEL, bn), lambda j: (0, j)),
                  pl.BlockSpec((1, bn), lambda j: (0, j))],
        out_specs=pl.BlockSpec((rows, bn), lambda j: (0, j)),
        out_shape=jax.ShapeDtypeStruct((rows, n), _F32),
        name="mod_proj",
    )(c_all, w_mod, b_mod)


def _pad_heads(v):
    return jnp.pad(jnp.tile(v, DT_COPIES), (0, DT_W - DT_COPIES * HEADS)).reshape(1, DT_W)


def kernel(x_prompt, x_sample, state_conv_a, state_conv_b, state_ssm, c_prompt, c_sample, w_mod, b_mod, norm_in_w, w_in, conv_a_w, norm_a_w, conv_b_w, conv_b_b, dt_bias, a_log, d_skip, norm_b_w, w_out, norm_f_w):
    depth = w_in.shape[0]
    assert depth == 1, "single-layer step"
    assert w_in.shape[2] == D_IN_PROJ
    nb, nd = x_prompt.shape[0], x_sample.shape[0]

    n_rows = nb + nd
    rows_pad = -(-n_rows // 16) * 16
    c_all = jnp.pad(jnp.concatenate([c_prompt, c_sample], axis=0), ((0, rows_pad - n_rows), (0, 0)))
    mod = _mod_call(c_all.astype(_BF16), w_mod[0], b_mod[0].reshape(1, -1))
    mod3 = mod[:n_rows].reshape(n_rows, 3, D_MODEL)

    w_in_b = w_in[0].astype(_BF16)
    w_dt = w_in_b[:, OFF_DT:]
    w_dt3 = jnp.concatenate([w_dt] * DT_COPIES + [jnp.zeros((D_MODEL, DT_W - DT_COPIES * HEADS), _BF16)], axis=1)
    params = (
        norm_in_w[0].reshape(1, D_MODEL), w_in_b, w_dt3, conv_a_w[0], norm_a_w[0].reshape(1, D_A),
        conv_b_w[0], conv_b_b[0].reshape(1, D_XBC), _pad_heads(dt_bias[0]), _pad_heads(a_log[0]),
        jnp.repeat(d_skip[0], HEADDIM).reshape(1, D_B), norm_b_w[0].reshape(1, D_B),
        w_out[0].astype(_BF16), norm_f_w.reshape(1, D_MODEL),
    )

    f32 = x_prompt.dtype
    za = jnp.zeros((nb, CONV_A_W - 1, D_A), f32)
    zb = jnp.zeros((nb, CONV_B_W - 1, D_XBC), f32)
    zs = jnp.zeros((nb, HEADS * HEADDIM, D_STATE), state_ssm.dtype)
    y_p, ca_p, cb_p, ss_p = _layer_call(x_prompt, mod3[:nb], za, zb, zs, params,
                                        S=1, Lb=PROMPT_BLOCK_ROWS, Q=PROMPT_SSD_CHUNK)
    y_s, ca_s, cb_s, ss_s = _layer_call(
        x_sample, mod3[nb:], state_conv_a[0], state_conv_b[0],
        state_ssm[0].reshape(nd, HEADS * HEADDIM, D_STATE), params,
        S=SAMPLE_SEQS_PER_STEP, Lb=x_sample.shape[1], Q=x_sample.shape[1])

    shp = lambda a: a.reshape(1, a.shape[0], HEADS, HEADDIM, D_STATE)
    return (y_p, y_s, ca_p[None], cb_p[None], shp(ss_p), ca_s[None], cb_s[None], shp(ss_s))
```

```python
import functools

import jax
import jax.numpy as jnp
import numpy as np
from jax import lax
from jax.experimental import pallas as pl
from jax.experimental.pallas import tpu as pltpu

D_MODEL = 1024
D_A = 1024
D_B = 1024
CONV_A_W = 3
CONV_B_W = 4
HEADS = 16
HEADDIM = 64
GROUPS = 4
HEADS_PER_GROUP = HEADS // GROUPS
D_STATE = 128
D_BC = GROUPS * D_STATE
D_XBC = D_B + 2 * D_BC
D_IN_PROJ = 4 * D_A + D_B + D_XBC + HEADS
GROUP_W = HEADS_PER_GROUP * HEADDIM
EPS = 1e-5
LOG2_E = 1.4426950408889634

LANES = 128
MXU_W = 256
HIST = 8
DT_COPIES = 3
DT_W = LANES
ROW_PAD = LANES

OFF_A = 0
OFF_Z = 4 * D_A
OFF_XBC = OFF_Z + D_B
OFF_DT = OFF_XBC + D_XBC

VMEM_LIMIT_BYTES = 56 * 1024 * 1024

PROMPT_BLOCK_ROWS = 256
PROMPT_SSD_CHUNK = 128
SAMPLE_SEQS_PER_STEP = 4

_F32 = jnp.float32
_BF16 = jnp.bfloat16


def _dot(a, b):
    return jnp.dot(a, b, preferred_element_type=_F32)


def _dot_nt(a, b):
    return lax.dot_general(a, b, (((1,), (1,)), ((), ())), preferred_element_type=_F32)


def _dot_tn(a, b):
    return lax.dot_general(a, b, (((0,), (0,)), ((), ())), preferred_element_type=_F32)


def _rms_scale(v):
    return lax.rsqrt(jnp.mean(v * v, axis=-1, keepdims=True) + EPS)


def _split3(v):
    hi = v.astype(_BF16)
    r1 = v - hi.astype(_F32)
    mid = r1.astype(_BF16)
    lo = (r1 - mid.astype(_F32)).astype(_BF16)
    lane = lax.broadcasted_iota(jnp.int32, v.shape, 1)
    return jnp.where(lane < HEADS, hi, jnp.where(lane < 2 * HEADS, mid, lo))


def _mod_kernel(c_ref, w_ref, b_ref, o_ref):
    o_ref[...] = _dot(c_ref[...], w_ref[...].astype(_BF16)) + b_ref[...]


def _layer_kernel(x_ref, mod_ref, sa_ref, sb_ref, ss_ref,
                  nin_ref, win_ref, wdt_ref, caw_ref, naw_ref, cbw_ref, cbb_ref, dtb_ref, alog_ref,
                  dsk_ref, nbw_ref, wout_ref, nf_ref, e64_ref, tri_ref,
                  y_ref, oa_ref, ob_ref, os_ref,
                  ubuf, xbuf, st_ref, z_ref, outa_ref, *, S, Lb, Q):
    t = pl.program_id(1)
    R = S * Lb

    @pl.when(t == 0)
    def _load_carried_state():
        for s in range(S):
            for hist_ref, state_ref, width in ((ubuf, sa_ref, CONV_A_W), (xbuf, sb_ref, CONV_B_W)):
                hist_ref[s] = jnp.zeros(hist_ref.shape[1:], _F32)
                hist_ref[s, HIST - (width - 1):HIST, :] = state_ref[s]
            for g in range(GROUPS):
                st_ref[s, g] = ss_ref[s, GROUP_W * g:GROUP_W * (g + 1), :].T

    def per_seq(fn):
        parts = [fn(s, slice(s * Lb, (s + 1) * Lb)) for s in range(S)]
        return parts[0] if S == 1 else jnp.concatenate(parts, axis=0)

    x = x_ref[...].reshape(R, D_MODEL)

    xn = x * _rms_scale(x)
    hb = per_seq(lambda s, rows: xn[rows] * (nin_ref[...] * (1.0 + mod_ref[s, 1:2, :])) + mod_ref[s, 0:1, :]).astype(_BF16)

    def causal_conv(val, hist_ref, s, w_ref):
        width = w_ref.shape[0]
        ext = jnp.concatenate([hist_ref[s], val], axis=0)
        acc = val * w_ref[width - 1:width, :]
        for d in range(1, width):
            acc = acc + pltpu.roll(ext, d, axis=0)[HIST:HIST + Lb] * w_ref[width - 1 - d:width - d, :]
        hist_ref[s] = val[Lb - HIST:Lb]
        return acc

    xr = _dot(hb, win_ref[:, OFF_XBC:OFF_XBC + D_XBC])
    dtr = _dot(hb, wdt_ref[...])
    xbc = jax.nn.silu(per_seq(lambda s, rows: causal_conv(xr[rows], xbuf, s, cbw_ref)) + cbb_ref[...])
    xs = xbc[:, 0:D_B]
    bm = xbc[:, D_B:D_B + D_BC].astype(_BF16)
    cm = xbc[:, D_B + D_BC:D_XBC].astype(_BF16)

    pa_ch = _dot(hb, win_ref[:, OFF_A + D_A:OFF_A + 3 * D_A])
    pa_b = _dot(hb, win_ref[:, OFF_A:OFF_A + D_A])
    pa_g = _dot(hb, win_ref[:, OFF_A + 3 * D_A:OFF_A + 4 * D_A])
    cb_all = {}
    for s in range(S):
        for c in range(Lb // Q):
            rows = slice(s * Lb + c * Q, s * Lb + (c + 1) * Q)
            for g in range(GROUPS):
                gn = slice(D_STATE * g, D_STATE * (g + 1))
                cb_all[s, c, g] = _dot_nt(cm[rows, gn], bm[rows, gn])
    dt = jax.nn.softplus(dtr + dtb_ref[...])
    da = dt * (-jnp.exp(alog_ref[...]))
    tri = tri_ref[...]
    da_hi = da.astype(_BF16)
    da_r1 = da - da_hi.astype(_F32)
    da_mid = da_r1.astype(_BF16)
    da_lo = (da_r1 - da_mid.astype(_F32)).astype(_BF16)
    acum = _dot(tri, da_hi) + _dot(tri, da_mid) + _dot(tri, da_lo)
    nchunks = R // Q
    alast = jnp.broadcast_to(acum.reshape(nchunks, Q, LANES)[:, Q - 1:Q, :],
                             (nchunks, Q, LANES)).reshape(R, LANES)
    ea_x = _dot(_split3(jnp.exp(acum)), e64_ref[...])
    w_x = _dot(_split3(dt * jnp.exp(alast - acum)), e64_ref[...])
    acum2 = acum * LOG2_E
    acum_t = acum2.T
    dt_t = dt.T

    b_gate = pa_b
    u = pa_ch[:, 0:D_A] * pa_ch[:, D_A:2 * D_A]
    g_a = pa_g
    va = b_gate * per_seq(lambda s, rows: causal_conv(u[rows], ubuf, s, caw_ref)) * jax.nn.silu(g_a)
    ya = (va * _rms_scale(va) * naw_ref[...]).astype(_BF16)

    def z_chunk(c):
        z_ref[:, c:c + MXU_W] = _dot(hb, win_ref[:, OFF_Z + c:OFF_Z + c + MXU_W])

    def outa_chunk(c):
        cols = slice(c, c + MXU_W)
        out_a = _dot(ya, wout_ref[0:D_A, cols])
        outa_ref[:, cols] = per_seq(lambda s, rows: x_ref[s, :, cols] + mod_ref[s, 2:3, cols] * out_a[rows])

    fillers = ([functools.partial(z_chunk, c) for c in range(0, D_B, MXU_W)]
               + [functools.partial(outa_chunk, c) for c in range(0, D_MODEL, MXU_W)])

    row_i = lax.broadcasted_iota(jnp.int32, (Q, HEADS_PER_GROUP * Q), 0)
    col_i = lax.broadcasted_iota(jnp.int32, (Q, HEADS_PER_GROUP * Q), 1)
    causal = row_i >= (col_i % Q)
    bd_row = lax.broadcasted_iota(jnp.int32, (HEADS_PER_GROUP * Q, GROUP_W), 0) // Q
    bd_col = lax.broadcasted_iota(jnp.int32, (HEADS_PER_GROUP * Q, GROUP_W), 1) // HEADDIM
    blockdiag = bd_row == bd_col

    y_rows = []
    for s in range(S):
        for c in range(Lb // Q):
            r0 = s * Lb + c * Q
            rows = slice(r0, r0 + Q)
            y_groups = []
            for g in range(GROUPS):
                gx = slice(GROUP_W * g, GROUP_W * (g + 1))
                gn = slice(D_STATE * g, D_STATE * (g + 1))
                heads = range(HEADS_PER_GROUP * g, HEADS_PER_GROUP * (g + 1))
                cg = cm[rows, gn]
                bg = bm[rows, gn]
                cb4 = jnp.concatenate([cb_all[s, c, g].astype(_BF16)] * HEADS_PER_GROUP, axis=1)
                arow = jnp.concatenate([acum_t[h:h + 1, rows] for h in heads], axis=1)
                dtrow = jnp.concatenate([dt_t[h:h + 1, rows] for h in heads], axis=1).astype(_BF16)
                acol = jnp.concatenate([jnp.broadcast_to(acum2[rows, h:h + 1], (Q, Q)) for h in heads], axis=1)
                seg = acol - arow
                mp = cb4 * jnp.exp2(jnp.where(causal, seg, -jnp.inf)).astype(_BF16) * dtrow
                xg = xs[rows, gx]
                xg_b = xg.astype(_BF16)
                bd = jnp.where(blockdiag, jnp.concatenate([xg_b] * HEADS_PER_GROUP, axis=0),
                               jnp.zeros((), _BF16))
                y_diag = _dot(mp, bd)
                st = st_ref[s, g]
                y_off = _dot(cg, st.astype(_BF16)) * ea_x[rows, gx]
                y_groups.append(y_diag + y_off)
                xw = (xg * w_x[rows, gx]).astype(_BF16)
                st_ref[s, g] = st * ea_x[r0 + Q - 1:r0 + Q, gx] + _dot_tn(bg, xw)
                if fillers:
                    fillers.pop(0)()
            y_rows.append(jnp.concatenate(y_groups, axis=1))
    y = y_rows[0] if len(y_rows) == 1 else jnp.concatenate(y_rows, axis=0)

    while fillers:
        fillers.pop(0)()
    y = y + dsk_ref[...] * xs
    vb = y * jax.nn.silu(z_ref[:, 0:D_B])
    yb = (vb * _rms_scale(vb) * nbw_ref[...]).astype(_BF16)

    out_b = _dot(yb, wout_ref[D_A:D_A + D_B, :])
    res = outa_ref[:, 0:D_MODEL] + per_seq(lambda s, rows: mod_ref[s, 2:3, :] * out_b[rows])
    y_ref[...] = (res * _rms_scale(res) * nf_ref[...]).reshape(S, Lb, D_MODEL)

    @pl.when(t == pl.num_programs(1) - 1)
    def _store_carried_state():
        for s in range(S):
            oa_ref[s] = ubuf[s, HIST - (CONV_A_W - 1):HIST, :]
            ob_ref[s] = xbuf[s, HIST - (CONV_B_W - 1):HIST, :]
            for g in range(GROUPS):
                os_ref[s, GROUP_W * g:GROUP_W * (g + 1), :] = st_ref[s, g].T


def _expand_matrix(lanes_per_head):
    e = np.zeros((LANES, HEADS * lanes_per_head), np.float32)
    for k in range(DT_COPIES):
        for h in range(HEADS):
            e[k * HEADS + h, h * lanes_per_head:(h + 1) * lanes_per_head] = 1.0
    return jnp.asarray(e, _BF16)


def _chunk_tri(rows, q):
    i = np.arange(rows)
    return jnp.asarray((i[:, None] // q == i[None, :] // q) & (i[None, :] <= i[:, None]), _BF16)


def _const_spec(shape):
    return pl.BlockSpec(shape, lambda i, j: (0,) * len(shape), pipeline_mode=pl.Buffered(1))


def _layer_call(x, mod3, sa, sb, ss, params, *, S, Lb, Q):
    ns, L, _ = x.shape
    R = S * Lb
    consts = (_expand_matrix(HEADDIM), _chunk_tri(R, Q))
    seq3 = lambda i, j: (i, 0, 0)
    in_specs = [
        pl.BlockSpec((S, Lb, D_MODEL), lambda i, j: (i, j, 0)),
        pl.BlockSpec((S, 3, D_MODEL), seq3),
        pl.BlockSpec((S, CONV_A_W - 1, D_A), seq3),
        pl.BlockSpec((S, CONV_B_W - 1, D_XBC), seq3),
        pl.BlockSpec((S, HEADS * HEADDIM, D_STATE), seq3),
    ] + [_const_spec(p.shape) for p in params + consts]
    out_shape = (
        jax.ShapeDtypeStruct((ns, L, D_MODEL), _F32),
        jax.ShapeDtypeStruct((ns, CONV_A_W - 1, D_A), _F32),
        jax.ShapeDtypeStruct((ns, CONV_B_W - 1, D_XBC), _F32),
        jax.ShapeDtypeStruct((ns, HEADS * HEADDIM, D_STATE), _F32),
    )
    out_specs = (
        pl.BlockSpec((S, Lb, D_MODEL), lambda i, j: (i, j, 0)),
        pl.BlockSpec((S, CONV_A_W - 1, D_A), seq3),
        pl.BlockSpec((S, CONV_B_W - 1, D_XBC), seq3),
        pl.BlockSpec((S, HEADS * HEADDIM, D_STATE), seq3),
    )
    scratch = [
        pltpu.VMEM((S, HIST, D_A), _F32),
        pltpu.VMEM((S, HIST, D_XBC), _F32),
        pltpu.VMEM((S, GROUPS, D_STATE, GROUP_W), _F32),
        pltpu.VMEM((R, D_B + ROW_PAD), _F32),
        pltpu.VMEM((R, D_MODEL + ROW_PAD), _F32),
    ]
    return pl.pallas_call(
        functools.partial(_layer_kernel, S=S, Lb=Lb, Q=Q),
        grid=(ns // S, L // Lb),
        in_specs=in_specs,
        out_specs=out_specs,
        out_shape=out_shape,
        scratch_shapes=scratch,
        compiler_params=pltpu.CompilerParams(
            dimension_semantics=("arbitrary", "arbitrary"),
            vmem_limit_bytes=VMEM_LIMIT_BYTES),
        name=f"layer_s{S}_l{Lb}",
    )(x, mod3, sa, sb, ss, *params, *consts)


def _mod_call(c_all, w_mod, b_mod):
    rows = c_all.shape[0]
    n = w_mod.shape[1]
    bn = D_MODEL
    return pl.pallas_call(
        _mod_kernel,
        grid=(n // bn,),
        in_specs=[pl.BlockSpec((rows, D_MODEL), lambda j: (0, 0)),
                  pl.BlockSpec((D_MODEL, bn), lambda j: (0, j)),
                  pl.BlockSpec((1, bn), lambda j: (0, j))],
        out_specs=pl.BlockSpec((rows, bn), lambda j: (0, j)),
        out_shape=jax.ShapeDtypeStruct((rows, n), _F32),
        name="mod_proj",
    )(c_all, w_mod, b_mod)


def _pad_heads(v):
    return jnp.pad(jnp.tile(v, DT_COPIES), (0, DT_W - DT_COPIES * HEADS)).reshape(1, DT_W)


def kernel(x_prompt, x_sample, state_conv_a, state_conv_b, state_ssm, c_prompt, c_sample, w_mod, b_mod, norm_in_w, w_in, conv_a_w, norm_a_w, conv_b_w, conv_b_b, dt_bias, a_log, d_skip, norm_b_w, w_out, norm_f_w):
    depth = w_in.shape[0]
    assert depth == 1, "single-layer step"
    assert w_in.shape[2] == D_IN_PROJ
    nb, nd = x_prompt.shape[0], x_sample.shape[0]

    n_rows = nb + nd
    rows_pad = -(-n_rows // 16) * 16
    c_all = jnp.pad(jnp.concatenate([c_prompt, c_sample], axis=0), ((0, rows_pad - n_rows), (0, 0)))
    mod = _mod_call(c_all.astype(_BF16), w_mod[0], b_mod[0].reshape(1, -1))
    mod3 = mod[:n_rows].reshape(n_rows, 3, D_MODEL)

    w_in_b = w_in[0].astype(_BF16)
    w_dt = w_in_b[:, OFF_DT:]
    w_dt3 = jnp.concatenate([w_dt] * DT_COPIES + [jnp.zeros((D_MODEL, DT_W - DT_COPIES * HEADS), _BF16)], axis=1)
    params = (
        norm_in_w[0].reshape(1, D_MODEL), w_in_b, w_dt3, conv_a_w[0], norm_a_w[0].reshape(1, D_A),
        conv_b_w[0], conv_b_b[0].reshape(1, D_XBC), _pad_heads(dt_bias[0]), _pad_heads(a_log[0]),
        jnp.repeat(d_skip[0], HEADDIM).reshape(1, D_B), norm_b_w[0].reshape(1, D_B),
        w_out[0].astype(_BF16), norm_f_w.reshape(1, D_MODEL),
    )

    f32 = x_prompt.dtype
    za = jnp.zeros((nb, CONV_A_W - 1, D_A), f32)
    zb = jnp.zeros((nb, CONV_B_W - 1, D_XBC), f32)
    zs = jnp.zeros((nb, HEADS * HEADDIM, D_STATE), state_ssm.dtype)
    y_p, ca_p, cb_p, ss_p = _layer_call(x_prompt, mod3[:nb], za, zb, zs, params,
                                        S=1, Lb=PROMPT_BLOCK_ROWS, Q=PROMPT_SSD_CHUNK)
    y_s, ca_s, cb_s, ss_s = _layer_call(
        x_sample, mod3[nb:], state_conv_a[0], state_conv_b[0],
        state_ssm[0].reshape(nd, HEADS * HEADDIM, D_STATE), params,
        S=SAMPLE_SEQS_PER_STEP, Lb=x_sample.shape[1], Q=x_sample.shape[1])

    shp = lambda a: a.reshape(1, a.shape[0], HEADS, HEADDIM, D_STATE)
    return (y_p, y_s, ca_p[None], cb_p[None], shp(ss_p), ca_s[None], cb_s[None], shp(ss_s))
```

```python
import functools

import jax
import jax.numpy as jnp
import numpy as np
from jax import lax
from jax.experimental import pallas as pl
from jax.experimental.pallas import tpu as pltpu

D_MODEL = 1024
D_A = 1024
D_B = 1024
CONV_A_W = 3
CONV_B_W = 4
HEADS = 16
HEADDIM = 64
GROUPS = 4
HEADS_PER_GROUP = HEADS // GROUPS
D_STATE = 128
D_BC = GROUPS * D_STATE
D_XBC = D_B + 2 * D_BC
D_IN_PROJ = 4 * D_A + D_B + D_XBC + HEADS
GROUP_W = HEADS_PER_GROUP * HEADDIM
EPS = 1e-5
LOG2_E = 1.4426950408889634

LANES = 128
MXU_W = 256
HIST = 8
DT_COPIES = 3
DT_W = LANES
ROW_PAD = LANES

OFF_A = 0
OFF_Z = 4 * D_A
OFF_XBC = OFF_Z + D_B
OFF_DT = OFF_XBC + D_XBC

VMEM_LIMIT_BYTES = 56 * 1024 * 1024

PROMPT_BLOCK_ROWS = 256
PROMPT_SSD_CHUNK = 128
SAMPLE_SEQS_PER_STEP = 4

_F32 = jnp.float32
_BF16 = jnp.bfloat16


def _dot(a, b):
    return jnp.dot(a, b, preferred_element_type=_F32)


def _dot_nt(a, b):
    return lax.dot_general(a, b, (((1,), (1,)), ((), ())), preferred_element_type=_F32)


def _dot_tn(a, b):
    return lax.dot_general(a, b, (((0,), (0,)), ((), ())), preferred_element_type=_F32)


def _rms_scale(v):
    return lax.rsqrt(jnp.mean(v * v, axis=-1, keepdims=True) + EPS)


def _split3(v):
    hi = v.astype(_BF16)
    r1 = v - hi.astype(_F32)
    mid = r1.astype(_BF16)
    lo = (r1 - mid.astype(_F32)).astype(_BF16)
    lane = lax.broadcasted_iota(jnp.int32, v.shape, 1)
    return jnp.where(lane < HEADS, hi, jnp.where(lane < 2 * HEADS, mid, lo))


def _mod_kernel(c_ref, w_ref, b_ref, o_ref):
    o_ref[...] = _dot(c_ref[...], w_ref[...].astype(_BF16)) + b_ref[...]


def _layer_kernel(x_ref, mod_ref, sa_ref, sb_ref, ss_ref,
                  nin_ref, win_ref, wdt_ref, caw_ref, naw_ref, cbw_ref, cbb_ref, dtb_ref, alog_ref,
                  dsk_ref, nbw_ref, wout_ref, nf_ref, e64_ref, tri_ref,
                  y_ref, oa_ref, ob_ref, os_ref,
                  ubuf, xbuf, st_ref, z_ref, outa_ref, *, S, Lb, Q):
    t = pl.program_id(1)
    R = S * Lb

    @pl.when(t == 0)
    def _load_carried_state():
        for s in range(S):
            for hist_ref, state_ref, width in ((ubuf, sa_ref, CONV_A_W), (xbuf, sb_ref, CONV_B_W)):
                hist_ref[s] = jnp.zeros(hist_ref.shape[1:], _F32)
                hist_ref[s, HIST - (width - 1):HIST, :] = state_ref[s]
            for g in range(GROUPS):
                st_ref[s, g] = ss_ref[s, GROUP_W * g:GROUP_W * (g + 1), :].T

    def per_seq(fn):
        parts = [fn(s, slice(s * Lb, (s + 1) * Lb)) for s in range(S)]
        return parts[0] if S == 1 else jnp.concatenate(parts, axis=0)

    x = x_ref[...].reshape(R, D_MODEL)

    xn = x * _rms_scale(x)
    hb = per_seq(lambda s, rows: xn[rows] * (nin_ref[...] * (1.0 + mod_ref[s, 1:2, :])) + mod_ref[s, 0:1, :]).astype(_BF16)

    def causal_conv(val, hist_ref, s, w_ref):
        width = w_ref.shape[0]
        ext = jnp.concatenate([hist_ref[s], val], axis=0)
        acc = val * w_ref[width - 1:width, :]
        for d in range(1, width):
            acc = acc + pltpu.roll(ext, d, axis=0)[HIST:HIST + Lb] * w_ref[width - 1 - d:width - d, :]
        hist_ref[s] = val[Lb - HIST:Lb]
        return acc

    xr = _dot(hb, win_ref[:, OFF_XBC:OFF_XBC + D_XBC])
    dtr = _dot(hb, wdt_ref[...])
    xbc = jax.nn.silu(per_seq(lambda s, rows: causal_conv(xr[rows], xbuf, s, cbw_ref)) + cbb_ref[...])
    xs = xbc[:, 0:D_B]
    bm = xbc[:, D_B:D_B + D_BC].astype(_BF16)
    cm = xbc[:, D_B + D_BC:D_XBC].astype(_BF16)

    pa_ch = _dot(hb, win_ref[:, OFF_A + D_A:OFF_A + 3 * D_A])
    pa_b = _dot(hb, win_ref[:, OFF_A:OFF_A + D_A])
    pa_g = _dot(hb, win_ref[:, OFF_A + 3 * D_A:OFF_A + 4 * D_A])
    cb_all = {}
    for s in range(S):
        for c in range(Lb // Q):
            rows = slice(s * Lb + c * Q, s * Lb + (c + 1) * Q)
            for g in range(GROUPS):
                gn = slice(D_STATE * g, D_STATE * (g + 1))
                cb_all[s, c, g] = _dot_nt(cm[rows, gn], bm[rows, gn])
    dt = jax.nn.softplus(dtr + dtb_ref[...])
    da = dt * (-jnp.exp(alog_ref[...]))
    tri = tri_ref[...]
    da_hi = da.astype(_BF16)
    da_r1 = da - da_hi.astype(_F32)
    da_mid = da_r1.astype(_BF16)
    da_lo = (da_r1 - da_mid.astype(_F32)).astype(_BF16)
    acum = _dot(tri, da_hi) + _dot(tri, da_mid) + _dot(tri, da_lo)
    nchunks = R // Q
    alast = jnp.broadcast_to(acum.reshape(nchunks, Q, LANES)[:, Q - 1:Q, :],
                             (nchunks, Q, LANES)).reshape(R, LANES)
    ea_x = _dot(_split3(jnp.exp(acum)), e64_ref[...])
    w_x = _dot(_split3(dt * jnp.exp(alast - acum)), e64_ref[...])
    acum2 = acum * LOG2_E
    acum_t = acum2.T
    dt_t = dt.T

    b_gate = pa_b
    u = pa_ch[:, 0:D_A] * pa_ch[:, D_A:2 * D_A]
    g_a = pa_g
    va = b_gate * per_seq(lambda s, rows: causal_conv(u[rows], ubuf, s, caw_ref)) * jax.nn.silu(g_a)
    ya = (va * _rms_scale(va) * naw_ref[...]).astype(_BF16)

    def z_chunk(c):
        z_ref[:, c:c + MXU_W] = _dot(hb, win_ref[:, OFF_Z + c:OFF_Z + c + MXU_W])

    def outa_chunk(c):
        cols = slice(c, c + MXU_W)
        out_a = _dot(ya, wout_ref[0:D_A, cols])
        outa_ref[:, cols] = per_seq(lambda s, rows: x_ref[s, :, cols] + mod_ref[s, 2:3, cols] * out_a[rows])

    fillers = ([functools.partial(z_chunk, c) for c in range(0, D_B, MXU_W)]
               + [functools.partial(outa_chunk, c) for c in range(0, D_MODEL, MXU_W)])

    row_i = lax.broadcasted_iota(jnp.int32, (Q, HEADS_PER_GROUP * Q), 0)
    col_i = lax.broadcasted_iota(jnp.int32, (Q, HEADS_PER_GROUP * Q), 1)
    causal = row_i >= (col_i % Q)
    bd_row = lax.broadcasted_iota(jnp.int32, (HEADS_PER_GROUP * Q, GROUP_W), 0) // Q
    bd_col = lax.broadcasted_iota(jnp.int32, (HEADS_PER_GROUP * Q, GROUP_W), 1) // HEADDIM
    blockdiag = bd_row == bd_col

    y_rows = []
    for s in range(S):
        for c in range(Lb // Q):
            r0 = s * Lb + c * Q
            rows = slice(r0, r0 + Q)
            y_groups = []
            for g in range(GROUPS):
                if fillers:
                    fillers.pop(0)()
                gx = slice(GROUP_W * g, GROUP_W * (g + 1))
                gn = slice(D_STATE * g, D_STATE * (g + 1))
                heads = range(HEADS_PER_GROUP * g, HEADS_PER_GROUP * (g + 1))
                cg = cm[rows, gn]
                bg = bm[rows, gn]
                cb4 = jnp.concatenate([cb_all[s, c, g]] * HEADS_PER_GROUP, axis=1)
                arow = jnp.concatenate([acum_t[h:h + 1, rows] for h in heads], axis=1)
                dtrow = jnp.concatenate([dt_t[h:h + 1, rows] for h in heads], axis=1)
                acol = jnp.concatenate([jnp.broadcast_to(acum2[rows, h:h + 1], (Q, Q)) for h in heads], axis=1)
                seg = acol - arow
                mp = cb4 * jnp.exp2(jnp.where(causal, seg, -jnp.inf)) * dtrow
                xg = xs[rows, gx]
                xg_b = xg.astype(_BF16)
                bd = jnp.where(blockdiag, jnp.concatenate([xg_b] * HEADS_PER_GROUP, axis=0),
                               jnp.zeros((), _BF16))
                y_diag = _dot(mp.astype(_BF16), bd)
                st = st_ref[s, g]
                y_off = _dot(cg, st.astype(_BF16)) * ea_x[rows, gx]
                y_groups.append(y_diag + y_off)
                xw = (xg * w_x[rows, gx]).astype(_BF16)
                st_ref[s, g] = st * ea_x[r0 + Q - 1:r0 + Q, gx] + _dot_tn(bg, xw)
            y_rows.append(jnp.concatenate(y_groups, axis=1))
    y = y_rows[0] if len(y_rows) == 1 else jnp.concatenate(y_rows, axis=0)

    while fillers:
        fillers.pop(0)()
    y = y + dsk_ref[...] * xs
    vb = y * jax.nn.silu(z_ref[:, 0:D_B])
    yb = (vb * _rms_scale(vb) * nbw_ref[...]).astype(_BF16)

    out_b = _dot(yb, wout_ref[D_A:D_A + D_B, :])
    res = outa_ref[:, 0:D_MODEL] + per_seq(lambda s, rows: mod_ref[s, 2:3, :] * out_b[rows])
    y_ref[...] = (res * _rms_scale(res) * nf_ref[...]).reshape(S, Lb, D_MODEL)

    @pl.when(t == pl.num_programs(1) - 1)
    def _store_carried_state():
        for s in range(S):
            oa_ref[s] = ubuf[s, HIST - (CONV_A_W - 1):HIST, :]
            ob_ref[s] = xbuf[s, HIST - (CONV_B_W - 1):HIST, :]
            for g in range(GROUPS):
                os_ref[s, GROUP_W * g:GROUP_W * (g + 1), :] = st_ref[s, g].T


def _expand_matrix(lanes_per_head):
    e = np.zeros((LANES, HEADS * lanes_per_head), np.float32)
    for k in range(DT_COPIES):
        for h in range(HEADS):
            e[k * HEADS + h, h * lanes_per_head:(h + 1) * lanes_per_head] = 1.0
    return jnp.asarray(e, _BF16)


def _chunk_tri(rows, q):
    i = np.arange(rows)
    return jnp.asarray((i[:, None] // q == i[None, :] // q) & (i[None, :] <= i[:, None]), _BF16)


def _const_spec(shape):
    return pl.BlockSpec(shape, lambda i, j: (0,) * len(shape), pipeline_mode=pl.Buffered(1))


def _layer_call(x, mod3, sa, sb, ss, params, *, S, Lb, Q):
    ns, L, _ = x.shape
    R = S * Lb
    consts = (_expand_matrix(HEADDIM), _chunk_tri(R, Q))
    seq3 = lambda i, j: (i, 0, 0)
    in_specs = [
        pl.BlockSpec((S, Lb, D_MODEL), lambda i, j: (i, j, 0)),
        pl.BlockSpec((S, 3, D_MODEL), seq3),
        pl.BlockSpec((S, CONV_A_W - 1, D_A), seq3),
        pl.BlockSpec((S, CONV_B_W - 1, D_XBC), seq3),
        pl.BlockSpec((S, HEADS * HEADDIM, D_STATE), seq3),
    ] + [_const_spec(p.shape) for p in params + consts]
    out_shape = (
        jax.ShapeDtypeStruct((ns, L, D_MODEL), _F32),
        jax.ShapeDtypeStruct((ns, CONV_A_W - 1, D_A), _F32),
        jax.ShapeDtypeStruct((ns, CONV_B_W - 1, D_XBC), _F32),
        jax.ShapeDtypeStruct((ns, HEADS * HEADDIM, D_STATE), _F32),
    )
    out_specs = (
        pl.BlockSpec((S, Lb, D_MODEL), lambda i, j: (i, j, 0)),
        pl.BlockSpec((S, CONV_A_W - 1, D_A), seq3),
        pl.BlockSpec((S, CONV_B_W - 1, D_XBC), seq3),
        pl.BlockSpec((S, HEADS * HEADDIM, D_STATE), seq3),
    )
    scratch = [
        pltpu.VMEM((S, HIST, D_A), _F32),
        pltpu.VMEM((S, HIST, D_XBC), _F32),
        pltpu.VMEM((S, GROUPS, D_STATE, GROUP_W), _F32),
        pltpu.VMEM((R, D_B + ROW_PAD), _F32),
        pltpu.VMEM((R, D_MODEL + ROW_PAD), _F32),
    ]
    return pl.pallas_call(
        functools.partial(_layer_kernel, S=S, Lb=Lb, Q=Q),
        grid=(ns // S, L // Lb),
        in_specs=in_specs,
        out_specs=out_specs,
        out_shape=out_shape,
        scratch_shapes=scratch,
        compiler_params=pltpu.CompilerParams(
            dimension_semantics=("arbitrary", "arbitrary"),
            vmem_limit_bytes=VMEM_LIMIT_BYTES),
        name=f"layer_s{S}_l{Lb}",
    )(x, mod3, sa, sb, ss, *params, *consts)


def _mod_call(c_all, w_mod, b_mod):
    rows = c_all.shape[0]
    n = w_mod.shape[1]
    bn = D_MODEL
    return pl.pallas_call(
        _mod_kernel,
        grid=(n // bn,),
        in_specs=[pl.BlockSpec((rows, D_MODEL), lambda j: (0, 0)),
                  pl.BlockSpec((D_MODEL, bn), lambda j: (0, j)),
                  pl.BlockSpec((1, bn), lambda j: (0, j))],
        out_specs=pl.BlockSpec((rows, bn), lambda j: (0, j)),
        out_shape=jax.ShapeDtypeStruct((rows, n), _F32),
        name="mod_proj",
    )(c_all, w_mod, b_mod)


def _pad_heads(v):
    return jnp.pad(jnp.tile(v, DT_COPIES), (0, DT_W - DT_COPIES * HEADS)).reshape(1, DT_W)


def kernel(x_prompt, x_sample, state_conv_a, state_conv_b, state_ssm, c_prompt, c_sample, w_mod, b_mod, norm_in_w, w_in, conv_a_w, norm_a_w, conv_b_w, conv_b_b, dt_bias, a_log, d_skip, norm_b_w, w_out, norm_f_w):
    depth = w_in.shape[0]
    assert depth == 1, "single-layer step"
    assert w_in.shape[2] == D_IN_PROJ
    nb, nd = x_prompt.shape[0], x_sample.shape[0]

    n_rows = nb + nd
    rows_pad = -(-n_rows // 16) * 16
    c_all = jnp.pad(jnp.concatenate([c_prompt, c_sample], axis=0), ((0, rows_pad - n_rows), (0, 0)))
    mod = _mod_call(c_all.astype(_BF16), w_mod[0], b_mod[0].reshape(1, -1))
    mod3 = mod[:n_rows].reshape(n_rows, 3, D_MODEL)

    w_in_b = w_in[0].astype(_BF16)
    w_dt = w_in_b[:, OFF_DT:]
    w_dt3 = jnp.concatenate([w_dt] * DT_COPIES + [jnp.zeros((D_MODEL, DT_W - DT_COPIES * HEADS), _BF16)], axis=1)
    params = (
        norm_in_w[0].reshape(1, D_MODEL), w_in_b, w_dt3, conv_a_w[0], norm_a_w[0].reshape(1, D_A),
        conv_b_w[0], conv_b_b[0].reshape(1, D_XBC), _pad_heads(dt_bias[0]), _pad_heads(a_log[0]),
        jnp.repeat(d_skip[0], HEADDIM).reshape(1, D_B), norm_b_w[0].reshape(1, D_B),
        w_out[0].astype(_BF16), norm_f_w.reshape(1, D_MODEL),
    )

    f32 = x_prompt.dtype
    za = jnp.zeros((nb, CONV_A_W - 1, D_A), f32)
    zb = jnp.zeros((nb, CONV_B_W - 1, D_XBC), f32)
    zs = jnp.zeros((nb, HEADS * HEADDIM, D_STATE), state_ssm.dtype)
    y_p, ca_p, cb_p, ss_p = _layer_call(x_prompt, mod3[:nb], za, zb, zs, params,
                                        S=1, Lb=PROMPT_BLOCK_ROWS, Q=PROMPT_SSD_CHUNK)
    y_s, ca_s, cb_s, ss_s = _layer_call(
        x_sample, mod3[nb:], state_conv_a[0], state_conv_b[0],
        state_ssm[0].reshape(nd, HEADS * HEADDIM, D_STATE), params,
        S=SAMPLE_SEQS_PER_STEP, Lb=x_sample.shape[1], Q=x_sample.shape[1])

    shp = lambda a: a.reshape(1, a.shape[0], HEADS, HEADDIM, D_STATE)
    return (y_p, y_s, ca_p[None], cb_p[None], shp(ss_p), ca_s[None], cb_s[None], shp(ss_s))
```
